```python
import jax, jax.numpy as jnp
from jax import lax
import numpy as np

D_MODEL = 1024
BATCH = 32
SEQ = 256
DEPTH = 4
DEC_BATCH = 8
DEC_SEQ = 2048
PAST_LEN = 256

GRID_W = 64
N_HEADS = 16
N_KV_HEADS = 4
HEAD_DIM = 64
Q_PER_KV = N_HEADS // N_KV_HEADS
ATTN_WIDTH = N_HEADS * HEAD_DIM
KV_WIDTH = N_KV_HEADS * HEAD_DIM
N_FOURIER_GROUPS = 4
FOURIER_GROUP_DIM = 128
FOURIER_WIDTH = N_FOURIER_GROUPS * FOURIER_GROUP_DIM
N_BRANCHES = 2
IN_WIDTH = FOURIER_WIDTH + ATTN_WIDTH + 2 * KV_WIDTH + N_BRANCHES * D_MODEL
WINDOW = 128
BLOCK = 128
D_FF = 4 * D_MODEL
ROPE_THETA = 10000.0
ROPE_PAIRS_PER_AXIS = HEAD_DIM // 4
EPS = 1e-6
NEG_INF = -1e30

kernel_name = "hybrid_fnet_swa_prefix_diffusion_step"


def rmsnorm(x, g):
    xf = x.astype(jnp.float32)
    y = xf * lax.rsqrt(jnp.mean(xf * xf, axis=-1, keepdims=True) + EPS)
    return (y * g.astype(jnp.float32)).astype(x.dtype)


def modulation(cvec, w_ada, b_ada):
    m = jax.nn.silu(cvec) @ w_ada + b_ada
    return [t[:, None, :] for t in jnp.split(m, 6, axis=-1)]


def axial_rope_tables(rows, n_tokens):
    t = jnp.arange(n_tokens)
    row = (t // GRID_W).astype(jnp.float32)
    col = (t % GRID_W).astype(jnp.float32)
    inv = ROPE_THETA ** (-jnp.arange(ROPE_PAIRS_PER_AXIS, dtype=jnp.float32) / ROPE_PAIRS_PER_AXIS)
    ang = jnp.concatenate([row[:, None] * inv[None, :], col[:, None] * inv[None, :]], axis=-1)
    return jnp.cos(ang), jnp.sin(ang)


def apply_rope(x, cos, sin):
    xf = x.astype(jnp.float32)
    x1, x2 = xf[..., 0::2], xf[..., 1::2]
    c, s = cos[None, :, None, :], sin[None, :, None, :]
    out = jnp.stack([x1 * c - x2 * s, x1 * s + x2 * c], axis=-1).reshape(x.shape)
    return out.astype(x.dtype)


def fourier_mix(u):
    b, s, _ = u.shape
    ug = u.astype(jnp.float32).reshape(b, s, N_FOURIER_GROUPS, FOURIER_GROUP_DIM)
    f = jnp.fft.fft2(ug, axes=(1, 3), norm="ortho").real
    return f.reshape(b, s, FOURIER_WIDTH).astype(u.dtype)


def attend_block(qb, k_all, v_all, sink, mask):
    scores = jnp.einsum('btkgd,blkd->bkgtl', qb, k_all,
                        preferred_element_type=jnp.float32) * (HEAD_DIM ** -0.5)
    if mask is not None:
        scores = jnp.where(mask[None, None, None], scores, NEG_INF)
    sink_col = jnp.broadcast_to(sink.astype(jnp.float32).reshape(N_KV_HEADS, Q_PER_KV)[None, :, :, None, None],
                                scores.shape[:-1] + (1,))
    p = jax.nn.softmax(jnp.concatenate([scores, sink_col], axis=-1), axis=-1)[..., :-1]
    return jnp.einsum('bkgtl,blkd->btkgd', p.astype(v_all.dtype), v_all)


def context_attention(q, k, v, sink):
    b, s = q.shape[:2]
    nb = s // BLOCK

    def one_block(i):
        qb = lax.dynamic_slice_in_dim(q, i * BLOCK, BLOCK, axis=1)
        return attend_block(qb, k, v, sink, None)

    out = lax.map(one_block, jnp.arange(nb))
    return jnp.moveaxis(out, 0, 1).reshape(b, s, ATTN_WIDTH)


def latent_attention(q, k, v, k_ctx, v_ctx, sink):
    b, s = q.shape[:2]
    nb = s // BLOCK
    k_pad = jnp.pad(k, ((0, 0), (BLOCK, BLOCK), (0, 0), (0, 0)))
    v_pad = jnp.pad(v, ((0, 0), (BLOCK, BLOCK), (0, 0), (0, 0)))
    q_off = jnp.arange(BLOCK)
    k_off = jnp.arange(3 * BLOCK) - BLOCK
    band = jnp.abs(k_off[None, :] - q_off[:, None]) <= WINDOW
    ctx_ok = jnp.ones((BLOCK, k_ctx.shape[1]), dtype=bool)

    def one_block(i):
        start = i * BLOCK
        qb = lax.dynamic_slice_in_dim(q, start, BLOCK, axis=1)
        kb = lax.dynamic_slice_in_dim(k_pad, start, 3 * BLOCK, axis=1)
        vb = lax.dynamic_slice_in_dim(v_pad, start, 3 * BLOCK, axis=1)
        kpos = start + k_off
        in_range = (kpos >= 0) & (kpos < s)
        mask = jnp.concatenate([band & in_range[None, :], ctx_ok], axis=-1)
        return attend_block(qb, jnp.concatenate([kb, k_ctx], axis=1),
                            jnp.concatenate([vb, v_ctx], axis=1), sink, mask)

    out = lax.map(one_block, jnp.arange(nb))
    return jnp.moveaxis(out, 0, 1).reshape(b, s, ATTN_WIDTH)


def split_projection(h, w_in):
    z = h @ w_in
    o1 = FOURIER_WIDTH
    o2 = o1 + ATTN_WIDTH
    o3 = o2 + KV_WIDTH
    o4 = o3 + KV_WIDTH
    return z[..., :o1], z[..., o1:o2], z[..., o2:o3], z[..., o3:o4], z[..., o4:]


def merge_branches(u_f, attn_o, g, w_fo, w_ao, w_out):
    gates = jax.nn.sigmoid(g)
    g_f, g_a = gates[..., :D_MODEL], gates[..., D_MODEL:]
    m = g_f * (fourier_mix(u_f) @ w_fo) + g_a * (attn_o @ w_ao)
    return m @ w_out


def channel_mixer(h, w_ff1, w_ff2):
    return jnp.square(jax.nn.relu(h @ w_ff1)) @ w_ff2


def context_layer(x, mod, g1, g2, w_in, sink, w_fo, w_ao, w_out, w_ff1, w_ff2):
    sh1, sc1, ga1, sh2, sc2, ga2 = mod
    b, s, _ = x.shape
    h = rmsnorm(x, g1) * (1 + sc1) + sh1
    u_f, q, k, v, g = split_projection(h, w_in)
    q = q.reshape(b, s, N_KV_HEADS, Q_PER_KV, HEAD_DIM)
    k = k.reshape(b, s, N_KV_HEADS, HEAD_DIM)
    v = v.reshape(b, s, N_KV_HEADS, HEAD_DIM)
    attn_o = context_attention(q, k, v, sink)
    x = x + ga1 * merge_branches(u_f, attn_o, g, w_fo, w_ao, w_out)
    h = rmsnorm(x, g2) * (1 + sc2) + sh2
    x = x + ga2 * channel_mixer(h, w_ff1, w_ff2)
    return x, k, v


def latent_layer(x, mod, k_ctx, v_ctx, cos, sin, g1, g2, w_in, sink, w_fo, w_ao, w_out, w_ff1, w_ff2):
    sh1, sc1, ga1, sh2, sc2, ga2 = mod
    b, s, _ = x.shape
    h = rmsnorm(x, g1) * (1 + sc1) + sh1
    u_f, q, k, v, g = split_projection(h, w_in)
    q = apply_rope(q.reshape(b, s, N_HEADS, HEAD_DIM), cos, sin).reshape(b, s, N_KV_HEADS, Q_PER_KV, HEAD_DIM)
    k = apply_rope(k.reshape(b, s, N_KV_HEADS, HEAD_DIM), cos, sin)
    v = v.reshape(b, s, N_KV_HEADS, HEAD_DIM)
    attn_o = latent_attention(q, k, v, k_ctx, v_ctx, sink)
    x = x + ga1 * merge_branches(u_f, attn_o, g, w_fo, w_ao, w_out)
    h = rmsnorm(x, g2) * (1 + sc2) + sh2
    x = x + ga2 * channel_mixer(h, w_ff1, w_ff2)
    return x


def setup_inputs(seed: int = 0) -> dict:
    key = jax.random.key(seed)
    ks = jax.random.split(key, 20)
    f32 = jnp.float32

    def w(k, shape, fan_in, scale=1.0):
        return jax.random.normal(k, shape, f32) * (scale * fan_in ** -0.5)

    return {
        "x_prompt": jax.random.normal(ks[0], (BATCH, SEQ, D_MODEL), f32),
        "x_sample": jax.random.normal(ks[1], (DEC_BATCH, DEC_SEQ, D_MODEL), f32),
        "c": jax.random.normal(ks[2], (DEC_BATCH, D_MODEL), f32),
        "cache_k": jax.random.normal(ks[3], (DEC_BATCH, DEPTH, PAST_LEN, N_KV_HEADS, HEAD_DIM), f32),
        "cache_v": jax.random.normal(ks[4], (DEC_BATCH, DEPTH, PAST_LEN, N_KV_HEADS, HEAD_DIM), f32),
        "c_ctx": jax.random.normal(ks[5], (D_MODEL,), f32),
        "w_ada": w(ks[6], (DEPTH, D_MODEL, 6 * D_MODEL), D_MODEL, 0.5),
        "b_ada": 0.01 * jax.random.normal(ks[7], (DEPTH, 6 * D_MODEL), f32),
        "norm1_g": 1.0 + 0.05 * jax.random.normal(ks[8], (DEPTH, D_MODEL), f32),
        "norm2_g": 1.0 + 0.05 * jax.random.normal(ks[9], (DEPTH, D_MODEL), f32),
        "w_in": w(ks[10], (DEPTH, D_MODEL, IN_WIDTH), D_MODEL),
        "sink": 0.5 * jax.random.normal(ks[11], (DEPTH, N_HEADS), f32),
        "w_fo": w(ks[12], (DEPTH, FOURIER_WIDTH, D_MODEL), FOURIER_WIDTH),
        "w_ao": w(ks[13], (DEPTH, ATTN_WIDTH, D_MODEL), ATTN_WIDTH),
        "w_out": w(ks[14], (DEPTH, D_MODEL, D_MODEL), D_MODEL),
        "w_ff1": w(ks[15], (DEPTH, D_MODEL, D_FF), D_MODEL),
        "w_ff2": w(ks[16], (DEPTH, D_FF, D_MODEL), D_FF),
        "final_g": 1.0 + 0.05 * jax.random.normal(ks[17], (D_MODEL,), f32),
    }


def reference(x_prompt, x_sample, c, cache_k, cache_v, c_ctx, w_ada, b_ada, norm1_g, norm2_g,
              w_in, sink, w_fo, w_ao, w_out, w_ff1, w_ff2, final_g):
    xc = x_prompt
    new_k, new_v = [], []
    for l in range(DEPTH):
        mod = modulation(c_ctx[None, :], w_ada[l], b_ada[l])
        xc, k_l, v_l = context_layer(xc, mod, norm1_g[l], norm2_g[l], w_in[l], sink[l],
                                     w_fo[l], w_ao[l], w_out[l], w_ff1[l], w_ff2[l])
        new_k.append(k_l)
        new_v.append(v_l)
    y_prompt = rmsnorm(xc, final_g)
    new_cache_k = jnp.stack(new_k, axis=1)
    new_cache_v = jnp.stack(new_v, axis=1)

    n_lat = x_sample.shape[1]
    rows = n_lat // GRID_W
    cos, sin = axial_rope_tables(rows, n_lat)
    xs = x_sample
    for l in range(DEPTH):
        mod = modulation(c, w_ada[l], b_ada[l])
        xs = latent_layer(xs, mod, cache_k[:, l], cache_v[:, l], cos, sin, norm1_g[l], norm2_g[l],
                          w_in[l], sink[l], w_fo[l], w_ao[l], w_out[l], w_ff1[l], w_ff2[l])
    y_sample = rmsnorm(xs, final_g)
    return (y_prompt, y_sample, new_cache_k, new_cache_v)
```

```python
import functools

import numpy as np
import jax
import jax.numpy as jnp
from jax import lax
from jax.experimental import pallas as pl
from jax.experimental.pallas import tpu as pltpu

D_MODEL = 1024
N_HEADS = 16
N_KV_HEADS = 4
HEAD_DIM = 64
Q_PER_KV = N_HEADS // N_KV_HEADS
ATTN_WIDTH = N_HEADS * HEAD_DIM
KV_WIDTH = N_KV_HEADS * HEAD_DIM
N_FOURIER_GROUPS = 4
FOURIER_GROUP_DIM = 128
FOURIER_WIDTH = N_FOURIER_GROUPS * FOURIER_GROUP_DIM
IN_WIDTH = FOURIER_WIDTH + ATTN_WIDTH + 2 * KV_WIDTH + 2 * D_MODEL
D_FF = 4 * D_MODEL
GRID_W = 64
WINDOW = 128
ROPE_THETA = 10000.0
ROPE_PAIRS_PER_AXIS = HEAD_DIM // 4
EPS = 1e-6
NEG_INF = -1e30
SCORE_SCALE = HEAD_DIM ** -0.5

OFF_Q = FOURIER_WIDTH
OFF_K = OFF_Q + ATTN_WIDTH
OFF_V = OFF_K + KV_WIDTH
OFF_G = OFF_V + KV_WIDTH

LANES = 128
MOD_ROWS = 16
VMEM_LIMIT_BYTES = 56 * 1024 * 1024
ROW_TILE = 512
COL_CHUNK = 512
Q_BLOCK = 128
KEY_SPAN = 3 * Q_BLOCK

BF16 = jnp.bfloat16
F32 = jnp.float32


def _params(n_axes):
    return pltpu.CompilerParams(dimension_semantics=("arbitrary",) * n_axes,
                                vmem_limit_bytes=VMEM_LIMIT_BYTES)


def _resident(shape):
    zeros = (0,) * len(shape)
    return pl.BlockSpec(shape, lambda *_: zeros, pipeline_mode=pl.Buffered(1))


def _dot(a, b):
    return jnp.dot(a, b, preferred_element_type=F32)


def _sigmoid(z):
    return 1.0 / (1.0 + jnp.exp(-z))


def _rmsnorm(x, g):
    return x * lax.rsqrt(jnp.mean(x * x, axis=-1, keepdims=True) + EPS) * g


def _mod_kernel(c_ref, w_ref, b_ref, o_ref):
    c = c_ref[...]
    a = c * _sigmoid(c)
    a_hi = a.astype(BF16)
    a_lo = (a - a_hi.astype(F32)).astype(BF16)
    w = w_ref[...]
    w_hi = w.astype(BF16)
    w_lo = (w - w_hi.astype(F32)).astype(BF16)
    o_ref[...] = _dot(a_hi, w_hi) + _dot(a_lo, w_hi) + _dot(a_hi, w_lo) + b_ref[...]


def _modulation(cvecs, w_ada, b_ada):
    depth, _, width = w_ada.shape
    tn = width // 4
    return pl.pallas_call(
        _mod_kernel,
        grid=(depth, width // tn),
        in_specs=[
            pl.BlockSpec((MOD_ROWS, D_MODEL), lambda l, j: (0, 0)),
            pl.BlockSpec((None, D_MODEL, tn), lambda l, j: (l, 0, j)),
            pl.BlockSpec((None, 1, tn), lambda l, j: (l, 0, j)),
        ],
        out_specs=pl.BlockSpec((None, MOD_ROWS, tn), lambda l, j: (l, 0, j)),
        out_shape=jax.ShapeDtypeStruct((depth, MOD_ROWS, width), F32),
        compiler_params=_params(2),
        name="modulation",
    )(cvecs, w_ada, b_ada.reshape(depth, 1, width))


def _mod_spec(layer, slot, row_of_tile):
    return pl.BlockSpec((None, None, None, 1, D_MODEL),
                        lambda i: (layer, row_of_tile(i), slot, 0, 0))


def _rope(z, cos, sin_even, sin_odd):
    outs = []
    for j in range(z.shape[1] // LANES):
        zj = z[:, j * LANES:(j + 1) * LANES]
        outs.append(zj * cos + pltpu.roll(zj, LANES - 1, 1) * sin_even + pltpu.roll(zj, 1, 1) * sin_odd)
    return outs[0] if len(outs) == 1 else jnp.concatenate(outs, axis=1)


def _inproj_kernel(*refs, rope):
    if rope:
        (x_ref, sh_ref, sc_ref, g_ref, w_ref, cos_ref, se_ref, so_ref,
         uf_ref, q_ref, k_ref, v_ref, gate_ref) = refs
        rot = lambda z: _rope(z, cos_ref[...], se_ref[...], so_ref[...])
    else:
        x_ref, sh_ref, sc_ref, g_ref, w_ref, uf_ref, q_ref, k_ref, v_ref, gate_ref = refs
        rot = lambda z: z
    h = _rmsnorm(x_ref[...], g_ref[...]) * (1.0 + sc_ref[...]) + sh_ref[...]
    h = h.astype(BF16)

    def seg(start):
        return _dot(h, w_ref[:, start:start + COL_CHUNK])

    uf_ref[...] = seg(0).astype(uf_ref.dtype)
    for j in range(ATTN_WIDTH // COL_CHUNK):
        z = rot(seg(OFF_Q + j * COL_CHUNK)) * SCORE_SCALE
        q_ref[:, j * COL_CHUNK:(j + 1) * COL_CHUNK] = z.astype(q_ref.dtype)
    z = seg(OFF_K)
    k_ref[...] = rot(z[:, :KV_WIDTH]).astype(k_ref.dtype)
    v_ref[...] = z[:, KV_WIDTH:].astype(v_ref.dtype)
    for j in range(2 * D_MODEL // COL_CHUNK):
        z = seg(OFF_G + j * COL_CHUNK)
        gate_ref[:, j * COL_CHUNK:(j + 1) * COL_CHUNK] = _sigmoid(z).astype(gate_ref.dtype)


def _in_projection(x, mod, layer, row_of_tile, g1, w_in, rope_tabs, kv_dtype, tm):
    rows = x.shape[0]
    tile = lambda w: pl.BlockSpec((tm, w), lambda i: (i, 0))
    in_specs = [tile(D_MODEL), _mod_spec(layer, 0, row_of_tile), _mod_spec(layer, 1, row_of_tile),
                _resident((1, D_MODEL)), _resident((D_MODEL, IN_WIDTH))]
    args = [x, mod, mod, g1, w_in]
    if rope_tabs is not None:
        tiles_per_seq = rope_tabs[0].shape[0] // tm
        in_specs += [pl.BlockSpec((tm, LANES), lambda i: (i % tiles_per_seq, 0))] * 3
        args += list(rope_tabs)
    widths = (FOURIER_WIDTH, ATTN_WIDTH, KV_WIDTH, KV_WIDTH, 2 * D_MODEL)
    dtypes = (BF16, BF16, kv_dtype, kv_dtype, BF16)
    return pl.pallas_call(
        functools.partial(_inproj_kernel, rope=rope_tabs is not None),
        grid=(rows // tm,),
        in_specs=in_specs,
        out_specs=[tile(w) for w in widths],
        out_shape=[jax.ShapeDtypeStruct((rows, w), dt) for w, dt in zip(widths, dtypes)],
        compiler_params=_params(1),
        name="in_projection",
    )(*args)


def _stack_heads(q, kv):
    base = kv * Q_PER_KV * HEAD_DIM
    return jnp.concatenate([q[:, base + g * HEAD_DIM: base + (g + 1) * HEAD_DIM] for g in range(Q_PER_KV)], axis=0)


def _sink_column(sink_ref, kv, t):
    return jnp.concatenate([jnp.full((t, 1), sink_ref[kv * Q_PER_KV + g], F32) for g in range(Q_PER_KV)], axis=0)


def _scores(qs, k):
    return lax.dot_general(qs, k, (((1,), (1,)), ((), ())), preferred_element_type=F32)


def _attn_ctx_kernel(sink_ref, q_ref, k_ref, v_ref, o_ref):
    t = q_ref.shape[0]
    q = q_ref[...]
    k = k_ref[...].astype(BF16)
    v = v_ref[...].astype(BF16)
    outs = []
    for kv in range(N_KV_HEADS):
        cols = slice(kv * HEAD_DIM, (kv + 1) * HEAD_DIM)
        s = _scores(_stack_heads(q, kv), k[:, cols])
        sink = _sink_column(sink_ref, kv, t)
        m = jnp.maximum(jnp.max(s, axis=1, keepdims=True), sink)
        p = jnp.exp(s - m)
        denom = jnp.sum(p, axis=1, keepdims=True) + jnp.exp(sink - m)
        o = _dot(p.astype(BF16), v[:, cols]) / denom
        outs += [o[g * t:(g + 1) * t] for g in range(Q_PER_KV)]
    o_ref[...] = jnp.concatenate(outs, axis=1).astype(o_ref.dtype)


def _context_attention(q, k, v, sink):
    b, s, _ = q.shape
    blk = lambda w: pl.BlockSpec((None, s, w), lambda i: (i, 0, 0))
    return pl.pallas_call(
        _attn_ctx_kernel,
        grid=(b,),
        in_specs=[pl.BlockSpec(memory_space=pltpu.SMEM), blk(ATTN_WIDTH), blk(KV_WIDTH), blk(KV_WIDTH)],
        out_specs=blk(ATTN_WIDTH),
        out_shape=jax.ShapeDtypeStruct((b, s, ATTN_WIDTH), BF16),
        compiler_params=_params(1),
        name="context_attention",
    )(sink, q, k, v)


def _attn_lat_kernel(sink_ref, q_ref, k_ref, v_ref, kc_ref, vc_ref, o_ref):
    t = q_ref.shape[0]
    s_len = k_ref.shape[0]
    i = pl.program_id(1)
    start = jnp.clip((i - 1) * t, 0, s_len - KEY_SPAN)
    start = pl.multiple_of(start, t)
    q = q_ref[...]
    kw = k_ref[pl.ds(start, KEY_SPAN), :]
    vw = v_ref[pl.ds(start, KEY_SPAN), :]
    kc = kc_ref[...].astype(BF16)
    vc = vc_ref[...].astype(BF16)
    rows = lax.broadcasted_iota(jnp.int32, (Q_PER_KV * t, KEY_SPAN), 0)
    cols = lax.broadcasted_iota(jnp.int32, (Q_PER_KV * t, KEY_SPAN), 1)
    qpos = i * t + (rows & (t - 1))
    in_window = jnp.abs(start + cols - qpos) <= WINDOW
    outs = []
    for kv in range(N_KV_HEADS):
        cs = slice(kv * HEAD_DIM, (kv + 1) * HEAD_DIM)
        qs = _stack_heads(q, kv)
        s_w = jnp.where(in_window, _scores(qs, kw[:, cs]), NEG_INF)
        s_c = _scores(qs, kc[:, cs])
        sink = _sink_column(sink_ref, kv, t)
        m = jnp.maximum(jnp.maximum(jnp.max(s_w, axis=1, keepdims=True), jnp.max(s_c, axis=1, keepdims=True)), sink)
        p_w = jnp.exp(s_w - m)
        p_c = jnp.exp(s_c - m)
        denom = jnp.sum(p_w, axis=1, keepdims=True) + jnp.sum(p_c, axis=1, keepdims=True) + jnp.exp(sink - m)
        o = (_dot(p_w.astype(BF16), vw[:, cs]) + _dot(p_c.astype(BF16), vc[:, cs])) / denom
        outs += [o[g * t:(g + 1) * t] for g in range(Q_PER_KV)]
    o_ref[...] = jnp.concatenate(outs, axis=1).astype(o_ref.dtype)


def _latent_attention(q, k, v, cache_k, cache_v, layer, sink):
    b, s, _ = q.shape
    past = cache_k.shape[2]
    assert Q_BLOCK & (Q_BLOCK - 1) == 0 and s % Q_BLOCK == 0 and s >= KEY_SPAN and WINDOW <= Q_BLOCK
    qblk = pl.BlockSpec((None, Q_BLOCK, ATTN_WIDTH), lambda bi, i: (bi, i, 0))
    seq = pl.BlockSpec((None, s, KV_WIDTH), lambda bi, i: (bi, 0, 0))
    cache = pl.BlockSpec((None, None, past, KV_WIDTH), lambda bi, i: (bi, layer, 0, 0))
    return pl.pallas_call(
        _attn_lat_kernel,
        grid=(b, s // Q_BLOCK),
        in_specs=[pl.BlockSpec(memory_space=pltpu.SMEM), qblk, seq, seq, cache, cache],
        out_specs=qblk,
        out_shape=jax.ShapeDtypeStruct((b, s, ATTN_WIDTH), BF16),
        compiler_params=_params(2),
        name="latent_attention",
    )(sink, q, k, v, cache_k, cache_v)


def _dft_tables(s):
    def cos_sin(n):
        idx = np.arange(n, dtype=np.int64)
        ang = 2.0 * np.pi * ((idx[:, None] * idx[None, :]) % n) / n
        scale = n ** -0.5
        return (np.cos(ang) * scale).astype(np.float32), (np.sin(ang) * scale).astype(np.float32)
    cc, sc = cos_sin(FOURIER_GROUP_DIM)
    cs, ss = cos_sin(s)
    chan = jnp.asarray(np.concatenate([cc, sc], axis=1)).astype(BF16)
    pos = jnp.asarray(np.concatenate([cs, -ss], axis=1)).astype(BF16)
    return chan, pos


def _fourier_kernel(x_ref, chan_ref, pos_ref, o_ref, ab_ref, *, row_chunk):
    nb, s, _ = x_ref.shape
    gd = FOURIER_GROUP_DIM
    for b in range(nb):
        for r in range(s // row_chunk):
            rs = slice(r * row_chunk, (r + 1) * row_chunk)
            for g in range(N_FOURIER_GROUPS):
                ab = _dot(x_ref[b, rs, g * gd:(g + 1) * gd], chan_ref[...])
                ab_ref[rs, g * gd:(g + 1) * gd] = ab[:, :gd].astype(BF16)
                ab_ref[s + r * row_chunk: s + (r + 1) * row_chunk, g * gd:(g + 1) * gd] = ab[:, gd:].astype(BF16)
        for r in range(s // row_chunk):
            rs = slice(r * row_chunk, (r + 1) * row_chunk)
            o_ref[b, rs, :] = _dot(pos_ref[rs, :], ab_ref[...]).astype(o_ref.dtype)


def _fourier_mix(u, tables, seqs_per_step):
    b, s, w = u.shape
    chan, pos = tables
    row_chunk = min(s, 512)
    blk = pl.BlockSpec((seqs_per_step, s, w), lambda i: (i, 0, 0))
    return pl.pallas_call(
        functools.partial(_fourier_kernel, row_chunk=row_chunk),
        grid=(b // seqs_per_step,),
        in_specs=[blk, _resident(chan.shape), _resident(pos.shape)],
        out_specs=blk,
        out_shape=jax.ShapeDtypeStruct((b, s, w), BF16),
        scratch_shapes=[pltpu.VMEM((2 * s, w), BF16)],
        compiler_params=_params(1),
        name="fourier_mix",
    )(u, chan, pos)


def _merge_mlp_kernel(*refs, final):
    (x_ref, f_ref, a_ref, gate_ref, ga1_ref, sh2_ref, sc2_ref, ga2_ref, g2_ref,
     wfo_ref, wao_ref, wout_ref, w1_ref, w2_ref) = refs[:14]
    o_ref = refs[-1]
    gate_f = gate_ref[:, :D_MODEL].astype(F32)
    gate_a = gate_ref[:, D_MODEL:].astype(F32)
    m = gate_f * _dot(f_ref[...], wfo_ref[...]) + gate_a * _dot(a_ref[...], wao_ref[...])
    x1 = x_ref[...] + ga1_ref[...] * _dot(m.astype(BF16), wout_ref[...])
    h = (_rmsnorm(x1, g2_ref[...]) * (1.0 + sc2_ref[...]) + sh2_ref[...]).astype(BF16)
    ff = None
    for j in range(D_FF // COL_CHUNK):
        cs = slice(j * COL_CHUNK, (j + 1) * COL_CHUNK)
        t = jnp.maximum(_dot(h, w1_ref[:, cs]), 0.0)
        part = _dot((t * t).astype(BF16), w2_ref[cs, :])
        ff = part if ff is None else ff + part
    x2 = x1 + ga2_ref[...] * ff
    if final:
        x2 = _rmsnorm(x2, refs[14][...])
    o_ref[...] = x2


def _merge_mlp(x, f, attn, gates, mod, layer, row_of_tile, g2, w_fo, w_ao, w_out, w_ff1, w_ff2, final_g, tm):
    rows = x.shape[0]
    tile = lambda w: pl.BlockSpec((tm, w), lambda i: (i, 0))
    in_specs = [tile(D_MODEL), tile(FOURIER_WIDTH), tile(ATTN_WIDTH), tile(2 * D_MODEL)]
    in_specs += [_mod_spec(layer, slot, row_of_tile) for slot in (2, 3, 4, 5)]
    in_specs += [_resident((1, D_MODEL))] + [_resident(w.shape) for w in (w_fo, w_ao, w_out, w_ff1, w_ff2)]
    args = [x, f, attn, gates, mod, mod, mod, mod, g2, w_fo, w_ao, w_out, w_ff1, w_ff2]
    if final_g is not None:
        in_specs.append(_resident((1, D_MODEL)))
        args.append(final_g)
    return pl.pallas_call(
        functools.partial(_merge_mlp_kernel, final=final_g is not None),
        grid=(rows // tm,),
        in_specs=in_specs,
        out_specs=tile(D_MODEL),
        out_shape=jax.ShapeDtypeStruct((rows, D_MODEL), F32),
        compiler_params=_params(1),
        name="merge_mlp",
    )(*args)


def _rope_tables(n_tokens):
    t = jnp.arange(n_tokens)
    row = (t // GRID_W).astype(F32)
    col = (t % GRID_W).astype(F32)
    inv = ROPE_THETA ** (-jnp.arange(ROPE_PAIRS_PER_AXIS, dtype=F32) / ROPE_PAIRS_PER_AXIS)
    ang = jnp.concatenate([row[:, None] * inv[None, :], col[:, None] * inv[None, :]], axis=-1)
    cos = jnp.repeat(jnp.cos(ang), 2, axis=1)
    sin = jnp.repeat(jnp.sin(ang), 2, axis=1)
    even = (jnp.arange(HEAD_DIM) % 2 == 0)[None, :]
    reps = LANES // HEAD_DIM
    return (jnp.tile(cos, (1, reps)),
            jnp.tile(jnp.where(even, -sin, 0.0), (1, reps)),
            jnp.tile(jnp.where(even, 0.0, sin), (1, reps)))


def kernel(x_prompt, x_sample, c, cache_k, cache_v, c_ctx, w_ada, b_ada, norm1_g, norm2_g,
           w_in, sink, w_fo, w_ao, w_out, w_ff1, w_ff2, final_g):
    depth = w_in.shape[0]
    b_ctx, s_ctx, _ = x_prompt.shape
    b_lat, s_lat, _ = x_sample.shape
    past = cache_k.shape[2]
    assert 1 + b_lat <= MOD_ROWS
    tm = ROW_TILE
    assert (b_ctx * s_ctx) % tm == 0 and s_lat % tm == 0

    cvecs = jnp.zeros((MOD_ROWS, D_MODEL), F32).at[0].set(c_ctx).at[1:1 + b_lat].set(c)
    mod = _modulation(cvecs, w_ada, b_ada).reshape(depth, MOD_ROWS, 6, 1, D_MODEL)
    ctx_row = lambda i: 0
    lat_row = lambda i: 1 + i // (s_lat // tm)

    w_in_b, w_fo_b, w_ao_b, w_out_b, w_ff1_b, w_ff2_b = (
        w.astype(BF16) for w in (w_in, w_fo, w_ao, w_out, w_ff1, w_ff2))
    g1 = norm1_g.reshape(depth, 1, D_MODEL)
    g2 = norm2_g.reshape(depth, 1, D_MODEL)
    gf = final_g.reshape(1, D_MODEL)
    rope_tabs = _rope_tables(s_lat)
    dft_ctx = _dft_tables(s_ctx)
    dft_lat = _dft_tables(s_lat)
    cache_k = cache_k.reshape(b_lat, depth, past, KV_WIDTH)
    cache_v = cache_v.reshape(b_lat, depth, past, KV_WIDTH)

    def layer_tail(x, uf, attn, gates, l, row_of_tile, fourier_tabs, batch, seqs_per_step):
        seq = x.shape[0] // batch
        f = _fourier_mix(uf.reshape(batch, seq, FOURIER_WIDTH), fourier_tabs, seqs_per_step)
        return _merge_mlp(x, f.reshape(-1, FOURIER_WIDTH), attn.reshape(-1, ATTN_WIDTH), gates, mod, l, row_of_tile,
                          g2[l], w_fo_b[l], w_ao_b[l], w_out_b[l], w_ff1_b[l], w_ff2_b[l],
                          gf if l == depth - 1 else None, tm)

    xc = x_prompt.reshape(b_ctx * s_ctx, D_MODEL)
    new_k, new_v = [], []
    for l in range(depth):
        uf, q, k, v, gates = _in_projection(xc, mod, l, ctx_row, g1[l], w_in_b[l], None, F32, tm)
        new_k.append(k.reshape(b_ctx, s_ctx, N_KV_HEADS, HEAD_DIM))
        new_v.append(v.reshape(b_ctx, s_ctx, N_KV_HEADS, HEAD_DIM))
        attn = _context_attention(q.reshape(b_ctx, s_ctx, ATTN_WIDTH), k.reshape(b_ctx, s_ctx, KV_WIDTH),
                                  v.reshape(b_ctx, s_ctx, KV_WIDTH), sink[l])
        xc = layer_tail(xc, uf, attn, gates, l, ctx_row, dft_ctx, b_ctx, 4 if b_ctx % 4 == 0 else 1)
    y_prompt = xc.reshape(b_ctx, s_ctx, D_MODEL)

    xs = x_sample.reshape(b_lat * s_lat, D_MODEL)
    for l in range(depth):
        uf, q, k, v, gates = _in_projection(xs, mod, l, lat_row, g1[l], w_in_b[l], rope_tabs, BF16, tm)
        attn = _latent_attention(q.reshape(b_lat, s_lat, ATTN_WIDTH), k.reshape(b_lat, s_lat, KV_WIDTH),
                                 v.reshape(b_lat, s_lat, KV_WIDTH), cache_k, cache_v, l, sink[l])
        xs = layer_tail(xs, uf, attn, gates, l, lat_row, dft_lat, b_lat, 1)
    y_sample = xs.reshape(b_lat, s_lat, D_MODEL)

    return y_prompt, y_sample, jnp.stack(new_k, axis=1), jnp.stack(new_v, axis=1)
```

```python
import functools

import numpy as np
import jax
import jax.numpy as jnp
from jax import lax
from jax.experimental import pallas as pl
from jax.experimental.pallas import tpu as pltpu

D_MODEL = 1024
N_HEADS = 16
N_KV_HEADS = 4
HEAD_DIM = 64
Q_PER_KV = N_HEADS // N_KV_HEADS
ATTN_WIDTH = N_HEADS * HEAD_DIM
KV_WIDTH = N_KV_HEADS * HEAD_DIM
N_FOURIER_GROUPS = 4
FOURIER_GROUP_DIM = 128
FOURIER_WIDTH = N_FOURIER_GROUPS * FOURIER_GROUP_DIM
IN_WIDTH = FOURIER_WIDTH + ATTN_WIDTH + 2 * KV_WIDTH + 2 * D_MODEL
D_FF = 4 * D_MODEL
GRID_W = 64
WINDOW = 128
ROPE_THETA = 10000.0
ROPE_PAIRS_PER_AXIS = HEAD_DIM // 4
EPS = 1e-6
NEG_INF = -1e30
LOG2E = 1.4426950408889634
QUERY_SCALE = HEAD_DIM ** -0.5 * LOG2E

OFF_Q = FOURIER_WIDTH
OFF_K = OFF_Q + ATTN_WIDTH
OFF_V = OFF_K + KV_WIDTH
OFF_G = OFF_V + KV_WIDTH

LANES = 128
MOD_ROWS = 16
VMEM_LIMIT_BYTES = 56 * 1024 * 1024
ROW_TILE = 512
COL_CHUNK = 512
Q_BLOCK = 128
KEY_SPAN = 3 * Q_BLOCK
SOFTMAX_VREGS = 40

BF16 = jnp.bfloat16
F32 = jnp.float32


def _params(n_axes):
    return pltpu.CompilerParams(dimension_semantics=("arbitrary",) * n_axes,
                                vmem_limit_bytes=VMEM_LIMIT_BYTES)


def _resident(shape):
    zeros = (0,) * len(shape)
    return pl.BlockSpec(shape, lambda *_: zeros, pipeline_mode=pl.Buffered(1))


def _dot(a, b):
    return jnp.dot(a, b, preferred_element_type=F32)


def _sigmoid(z):
    return 1.0 / (1.0 + jnp.exp(-z))


def _rmsnorm(x, g):
    return x * lax.rsqrt(jnp.mean(x * x, axis=-1, keepdims=True) + EPS) * g


def _mod_kernel(c_ref, w_ref, b_ref, o_ref):
    c = c_ref[...]
    a = c * _sigmoid(c)
    a_hi = a.astype(BF16)
    a_lo = (a - a_hi.astype(F32)).astype(BF16)
    w = w_ref[...]
    w_hi = w.astype(BF16)
    w_lo = (w - w_hi.astype(F32)).astype(BF16)
    o_ref[...] = _dot(a_hi, w_hi) + _dot(a_lo, w_hi) + _dot(a_hi, w_lo) + b_ref[...]


def _modulation(cvecs, w_ada, b_ada):
    depth, _, width = w_ada.shape
    tn = width // 4
    return pl.pallas_call(
        _mod_kernel,
        grid=(depth, width // tn),
        in_specs=[
            pl.BlockSpec((MOD_ROWS, D_MODEL), lambda l, j: (0, 0)),
            pl.BlockSpec((None, D_MODEL, tn), lambda l, j: (l, 0, j)),
            pl.BlockSpec((None, 1, tn), lambda l, j: (l, 0, j)),
        ],
        out_specs=pl.BlockSpec((None, MOD_ROWS, tn), lambda l, j: (l, 0, j)),
        out_shape=jax.ShapeDtypeStruct((depth, MOD_ROWS, width), F32),
        compiler_params=_params(2),
        name="modulation",
    )(cvecs, w_ada, b_ada.reshape(depth, 1, width))


def _mod_spec(layer, slot, row_of_tile):
    return pl.BlockSpec((None, None, None, 1, D_MODEL),
                        lambda i: (layer, row_of_tile(i), slot, 0, 0))


def _rope(z, cos, sin_even, sin_odd):
    outs = []
    for j in range(z.shape[1] // LANES):
        zj = z[:, j * LANES:(j + 1) * LANES]
        outs.append(zj * cos + pltpu.roll(zj, LANES - 1, 1) * sin_even + pltpu.roll(zj, 1, 1) * sin_odd)
    return outs[0] if len(outs) == 1 else jnp.concatenate(outs, axis=1)


def _augment_values(v):
    ones = jnp.ones((v.shape[0], HEAD_DIM), v.dtype)
    parts = []
    for kv in range(N_KV_HEADS):
        parts += [v[:, kv * HEAD_DIM:(kv + 1) * HEAD_DIM], ones]
    return jnp.concatenate(parts, axis=1).astype(BF16)


def _inproj_kernel(*refs, rope):
    if rope:
        (x_ref, sh_ref, sc_ref, g_ref, w_ref, cos_ref, se_ref, so_ref,
         uf_ref, q_ref, k_ref, v_ref, gate_ref) = refs
        rot = lambda z: _rope(z, cos_ref[...], se_ref[...], so_ref[...])
    else:
        x_ref, sh_ref, sc_ref, g_ref, w_ref, uf_ref, q_ref, k_ref, v_ref, gate_ref = refs
        rot = lambda z: z
    h = _rmsnorm(x_ref[...], g_ref[...]) * (1.0 + sc_ref[...]) + sh_ref[...]
    h = h.astype(BF16)

    def seg(start):
        return _dot(h, w_ref[:, start:start + COL_CHUNK])

    uf_ref[...] = seg(0).astype(uf_ref.dtype)
    for j in range(ATTN_WIDTH // COL_CHUNK):
        z = rot(seg(OFF_Q + j * COL_CHUNK)) * QUERY_SCALE
        q_ref[:, j * COL_CHUNK:(j + 1) * COL_CHUNK] = z.astype(q_ref.dtype)
    z = seg(OFF_K)
    k_ref[...] = rot(z[:, :KV_WIDTH]).astype(k_ref.dtype)
    v = z[:, KV_WIDTH:]
    v_ref[...] = v if v_ref.shape[1] == KV_WIDTH else _augment_values(v)
    for j in range(2 * D_MODEL // COL_CHUNK):
        z = seg(OFF_G + j * COL_CHUNK)
        gate_ref[:, j * COL_CHUNK:(j + 1) * COL_CHUNK] = _sigmoid(z).astype(gate_ref.dtype)


def _in_projection(x, mod, layer, row_of_tile, g1, w_in, rope_tabs, kv_dtype, tm):
    rows = x.shape[0]
    tile = lambda w: pl.BlockSpec((tm, w), lambda i: (i, 0))
    in_specs = [tile(D_MODEL), _mod_spec(layer, 0, row_of_tile), _mod_spec(layer, 1, row_of_tile),
                _resident((1, D_MODEL)), _resident((D_MODEL, IN_WIDTH))]
    args = [x, mod, mod, g1, w_in]
    if rope_tabs is not None:
        tiles_per_seq = rope_tabs[0].shape[0] // tm
        in_specs += [pl.BlockSpec((tm, LANES), lambda i: (i % tiles_per_seq, 0))] * 3
        args += list(rope_tabs)
    v_width = KV_WIDTH if kv_dtype == F32 else N_KV_HEADS * LANES
    widths = (FOURIER_WIDTH, ATTN_WIDTH, KV_WIDTH, v_width, 2 * D_MODEL)
    dtypes = (BF16, BF16, kv_dtype, kv_dtype, BF16)
    return pl.pallas_call(
        functools.partial(_inproj_kernel, rope=rope_tabs is not None),
        grid=(rows // tm,),
        in_specs=in_specs,
        out_specs=[tile(w) for w in widths],
        out_shape=[jax.ShapeDtypeStruct((rows, w), dt) for w, dt in zip(widths, dtypes)],
        compiler_params=_params(1),
        name="in_projection",
    )(*args)


def _softmax_kv_head(kv, s_ref, bias_of, n_biased, sink_ref, p_ref, t):
    n_blocks = s_ref.shape[1] // LANES
    rows = min(t, 1 << ((SOFTMAX_VREGS * 8 // n_blocks).bit_length() - 1))
    assert t % rows == 0
    cols = [slice(j * LANES, (j + 1) * LANES) for j in range(n_blocks)]
    chunks = [(g, r, slice(g * t + r * rows, g * t + (r + 1) * rows), slice(r * rows, (r + 1) * rows))
              for g in range(Q_PER_KV) for r in range(t // rows)]
    sinks = [sink_ref[kv * Q_PER_KV + g] * LOG2E for g in range(Q_PER_KV)]

    maxima = {}
    for g, r, rs, qrows in chunks:
        blocks = [s_ref[rs, c] + bias_of(qrows, c) if j < n_biased else s_ref[rs, c] for j, c in enumerate(cols)]
        maxima[g, r] = jnp.maximum(jnp.max(functools.reduce(jnp.maximum, blocks), axis=1, keepdims=True), sinks[g])
    for g, r, rs, qrows in chunks:
        for j, c in enumerate(cols):
            z = s_ref[rs, c] - maxima[g, r]
            if j < n_biased:
                z = z + bias_of(qrows, c)
            p_ref[rs, c] = jnp.exp2(z).astype(BF16)
    sink_terms = []
    for g in range(Q_PER_KV):
        m = jnp.concatenate([maxima[g, r] for r in range(t // rows)], axis=0) if t != rows else maxima[g, 0]
        sink_terms.append(jnp.exp2(sinks[g] - m))
    return sink_terms


def _attend(q_ref, k_of, v_of, bias_of, n_biased, sink_ref, s_ref, p_ref, o_ref):
    t = q_ref.shape[0]
    lane = lax.broadcasted_iota(jnp.int32, (t, LANES), 1)

    def score(kv):
        heads = [q_ref[:, (kv * Q_PER_KV + g) * HEAD_DIM:(kv * Q_PER_KV + g + 1) * HEAD_DIM] for g in range(Q_PER_KV)]
        s_ref[kv] = lax.dot_general(jnp.concatenate(heads, axis=0), k_of(kv), (((1,), (1,)), ((), ())),
                                    preferred_element_type=F32)

    score(0)
    for kv in range(N_KV_HEADS):
        if kv + 1 < N_KV_HEADS:
            score(kv + 1)
        sink_terms = _softmax_kv_head(kv, s_ref.at[kv], bias_of, n_biased, sink_ref, p_ref.at[kv], t)
        o_aug = _dot(p_ref[kv], v_of(kv))
        for pair in range(Q_PER_KV // 2):
            halves = []
            for g in (2 * pair, 2 * pair + 1):
                og = o_aug[g * t:(g + 1) * t]
                halves.append(og / (pltpu.roll(og, HEAD_DIM, 1) + sink_terms[g]))
            blk = jnp.where(lane < HEAD_DIM, halves[0], pltpu.roll(halves[1], HEAD_DIM, 1))
            col = (kv * (Q_PER_KV // 2) + pair) * LANES
            o_ref[:, col:col + LANES] = blk.astype(o_ref.dtype)


def _attn_ctx_kernel(sink_ref, q_ref, k_ref, v_ref, o_ref, s_ref, p_ref):
    k = k_ref[...].astype(BF16)
    v_aug = _augment_values(v_ref[...])
    _attend(q_ref, lambda kv: k[:, kv * HEAD_DIM:(kv + 1) * HEAD_DIM],
            lambda kv: v_aug[:, kv * LANES:(kv + 1) * LANES], None, 0, sink_ref, s_ref, p_ref, o_ref)


def _context_attention(q, k, v, sink):
    b, s, _ = q.shape
    blk = lambda w: pl.BlockSpec((None, s, w), lambda i: (i, 0, 0))
    return pl.pallas_call(
        _attn_ctx_kernel,
        grid=(b,),
        in_specs=[pl.BlockSpec(memory_space=pltpu.SMEM), blk(ATTN_WIDTH), blk(KV_WIDTH), blk(KV_WIDTH)],
        out_specs=blk(ATTN_WIDTH),
        out_shape=jax.ShapeDtypeStruct((b, s, ATTN_WIDTH), BF16),
        scratch_shapes=[pltpu.VMEM((N_KV_HEADS, Q_PER_KV * s, s), F32),
                        pltpu.VMEM((N_KV_HEADS, Q_PER_KV * s, s), BF16)],
        compiler_params=_params(1),
        name="context_attention",
    )(sink, q, k, v)


def _window_bias():
    r = np.arange(Q_BLOCK)[None, :, None]
    c = np.arange(KEY_SPAN)[None, None, :]
    p = np.arange(KEY_SPAN // Q_BLOCK)[:, None, None]
    return jnp.asarray(np.where(np.abs(c - (r + p * Q_BLOCK)) <= WINDOW, 0.0, NEG_INF).astype(np.float32))


def _attn_lat_kernel(sink_ref, q_ref, k_ref, v_ref, kc_ref, vc_ref, bias_ref, o_ref, kall_ref, vall_ref, s_ref, p_ref):
    t = q_ref.shape[0]
    s_len = k_ref.shape[0]
    i = pl.program_id(1)

    @pl.when(i == 0)
    def _():
        kall_ref[KEY_SPAN:, :] = kc_ref[...].astype(BF16)
        vall_ref[KEY_SPAN:, :] = _augment_values(vc_ref[...])

    start = pl.multiple_of(jnp.clip((i - 1) * t, 0, s_len - KEY_SPAN), t)
    kall_ref[:KEY_SPAN, :] = k_ref[pl.ds(start, KEY_SPAN), :]
    vall_ref[:KEY_SPAN, :] = v_ref[pl.ds(start, KEY_SPAN), :]
    pattern = lax.div(i * t - start, t)
    _attend(q_ref, lambda kv: kall_ref[:, kv * HEAD_DIM:(kv + 1) * HEAD_DIM],
            lambda kv: vall_ref[:, kv * LANES:(kv + 1) * LANES],
            lambda rows, cols: bias_ref[pattern, rows, cols], KEY_SPAN // LANES, sink_ref, s_ref, p_ref, o_ref)


def _latent_attention(q, k, v_aug, cache_k, cache_v, layer, sink):
    b, s, _ = q.shape
    past = cache_k.shape[2]
    assert s % Q_BLOCK == 0 and s >= KEY_SPAN and WINDOW <= Q_BLOCK and past % LANES == 0
    n_keys = KEY_SPAN + past
    qblk = pl.BlockSpec((None, Q_BLOCK, ATTN_WIDTH), lambda bi, i: (bi, i, 0))
    seq = lambda w: pl.BlockSpec((None, s, w), lambda bi, i: (bi, 0, 0))
    cache = pl.BlockSpec((None, None, past, KV_WIDTH), lambda bi, i: (bi, layer, 0, 0))
    bias = _window_bias()
    return pl.pallas_call(
        _attn_lat_kernel,
        grid=(b, s // Q_BLOCK),
        in_specs=[pl.BlockSpec(memory_space=pltpu.SMEM), qblk, seq(KV_WIDTH), seq(N_KV_HEADS * LANES), cache, cache,
                  _resident(bias.shape)],
        out_specs=qblk,
        out_shape=jax.ShapeDtypeStruct((b, s, ATTN_WIDTH), BF16),
        scratch_shapes=[pltpu.VMEM((n_keys, KV_WIDTH), BF16), pltpu.VMEM((n_keys, N_KV_HEADS * LANES), BF16),
                        pltpu.VMEM((N_KV_HEADS, Q_PER_KV * Q_BLOCK, n_keys), F32),
                        pltpu.VMEM((N_KV_HEADS, Q_PER_KV * Q_BLOCK, n_keys), BF16)],
        compiler_params=_params(2),
        name="latent_attention",
    )(sink, q, k, v_aug, cache_k, cache_v, bias)


def _dft_tables(s):
    def cos_sin(n):
        idx = np.arange(n, dtype=np.int64)
        ang = 2.0 * np.pi * ((idx[:, None] * idx[None, :]) % n) / n
        scale = n ** -0.5
        return (np.cos(ang) * scale).astype(np.float32), (np.sin(ang) * scale).astype(np.float32)
    cc, sc = cos_sin(FOURIER_GROUP_DIM)
    cs, ss = cos_sin(s)
    chan = jnp.asarray(np.concatenate([cc, sc], axis=1)).astype(BF16)
    pos = jnp.asarray(np.concatenate([cs, -ss], axis=1)).astype(BF16)
    return chan, pos


def _fourier_kernel(x_ref, chan_ref, pos_ref, o_ref, ab_ref, *, row_chunk):
    nb, s, _ = x_ref.shape
    gd = FOURIER_GROUP_DIM
    for b in range(nb):
        for r in range(s // row_chunk):
            rs = slice(r * row_chunk, (r + 1) * row_chunk)
            for g in range(N_FOURIER_GROUPS):
                ab = _dot(x_ref[b, rs, g * gd:(g + 1) * gd], chan_ref[...])
                ab_ref[rs, g * gd:(g + 1) * gd] = ab[:, :gd].astype(BF16)
                ab_ref[s + r * row_chunk: s + (r + 1) * row_chunk, g * gd:(g + 1) * gd] = ab[:, gd:].astype(BF16)
        for r in range(s // row_chunk):
            rs = slice(r * row_chunk, (r + 1) * row_chunk)
            o_ref[b, rs, :] = _dot(pos_ref[rs, :], ab_ref[...]).astype(o_ref.dtype)


def _fourier_mix(u, tables, seqs_per_step):
    b, s, w = u.shape
    chan, pos = tables
    row_chunk = min(s, 512)
    blk = pl.BlockSpec((seqs_per_step, s, w), lambda i: (i, 0, 0))
    return pl.pallas_call(
        functools.partial(_fourier_kernel, row_chunk=row_chunk),
        grid=(b // seqs_per_step,),
        in_specs=[blk, _resident(chan.shape), _resident(pos.shape)],
        out_specs=blk,
        out_shape=jax.ShapeDtypeStruct((b, s, w), BF16),
        scratch_shapes=[pltpu.VMEM((2 * s, w), BF16)],
        compiler_params=_params(1),
        name="fourier_mix",
    )(u, chan, pos)


def _merge_mlp_kernel(*refs, final):
    (x_ref, f_ref, a_ref, gate_ref, ga1_ref, sh2_ref, sc2_ref, ga2_ref, g2_ref,
     wfo_ref, wao_ref, wout_ref, w1_ref, w2_ref) = refs[:14]
    o_ref = refs[-1]
    gate_f = gate_ref[:, :D_MODEL].astype(F32)
    gate_a = gate_ref[:, D_MODEL:].astype(F32)
    m = gate_f * _dot(f_ref[...], wfo_ref[...]) + gate_a * _dot(a_ref[...], wao_ref[...])
    x1 = x_ref[...] + ga1_ref[...] * _dot(m.astype(BF16), wout_ref[...])
    h = (_rmsnorm(x1, g2_ref[...]) * (1.0 + sc2_ref[...]) + sh2_ref[...]).astype(BF16)
    ff = None
    for j in range(D_FF // COL_CHUNK):
        cs = slice(j * COL_CHUNK, (j + 1) * COL_CHUNK)
        t = jnp.maximum(_dot(h, w1_ref[:, cs]), 0.0)
        part = _dot((t * t).astype(BF16), w2_ref[cs, :])
        ff = part if ff is None else ff + part
    x2 = x1 + ga2_ref[...] * ff
    if final:
        x2 = _rmsnorm(x2, refs[14][...])
    o_ref[...] = x2


def _merge_mlp(x, f, attn, gates, mod, layer, row_of_tile, g2, w_fo, w_ao, w_out, w_ff1, w_ff2, final_g, tm):
    rows = x.shape[0]
    tile = lambda w: pl.BlockSpec((tm, w), lambda i: (i, 0))
    in_specs = [tile(D_MODEL), tile(FOURIER_WIDTH), tile(ATTN_WIDTH), tile(2 * D_MODEL)]
    in_specs += [_mod_spec(layer, slot, row_of_tile) for slot in (2, 3, 4, 5)]
    in_specs += [_resident((1, D_MODEL))] + [_resident(w.shape) for w in (w_fo, w_ao, w_out, w_ff1, w_ff2)]
    args = [x, f, attn, gates, mod, mod, mod, mod, g2, w_fo, w_ao, w_out, w_ff1, w_ff2]
    if final_g is not None:
        in_specs.append(_resident((1, D_MODEL)))
        args.append(final_g)
    return pl.pallas_call(
        functools.partial(_merge_mlp_kernel, final=final_g is not None),
        grid=(rows // tm,),
        in_specs=in_specs,
        out_specs=tile(D_MODEL),
        out_shape=jax.ShapeDtypeStruct((rows, D_MODEL), F32),
        compiler_params=_params(1),
        name="merge_mlp",
    )(*args)


def _rope_tables(n_tokens):
    t = jnp.arange(n_tokens)
    row = (t // GRID_W).astype(F32)
    col = (t % GRID_W).astype(F32)
    inv = ROPE_THETA ** (-jnp.arange(ROPE_PAIRS_PER_AXIS, dtype=F32) / ROPE_PAIRS_PER_AXIS)
    ang = jnp.concatenate([row[:, None] * inv[None, :], col[:, None] * inv[None, :]], axis=-1)
    cos = jnp.repeat(jnp.cos(ang), 2, axis=1)
    sin = jnp.repeat(jnp.sin(ang), 2, axis=1)
    even = (jnp.arange(HEAD_DIM) % 2 == 0)[None, :]
    reps = LANES // HEAD_DIM
    return (jnp.tile(cos, (1, reps)),
            jnp.tile(jnp.where(even, -sin, 0.0), (1, reps)),
            jnp.tile(jnp.where(even, 0.0, sin), (1, reps)))


def kernel(x_prompt, x_sample, c, cache_k, cache_v, c_ctx, w_ada, b_ada, norm1_g, norm2_g,
           w_in, sink, w_fo, w_ao, w_out, w_ff1, w_ff2, final_g):
    depth = w_in.shape[0]
    b_ctx, s_ctx, _ = x_prompt.shape
    b_lat, s_lat, _ = x_sample.shape
    past = cache_k.shape[2]
    assert 1 + b_lat <= MOD_ROWS
    tm = ROW_TILE
    assert (b_ctx * s_ctx) % tm == 0 and s_lat % tm == 0

    cvecs = jnp.zeros((MOD_ROWS, D_MODEL), F32).at[0].set(c_ctx).at[1:1 + b_lat].set(c)
    mod = _modulation(cvecs, w_ada, b_ada).reshape(depth, MOD_ROWS, 6, 1, D_MODEL)
    ctx_row = lambda i: 0
    lat_row = lambda i: 1 + i // (s_lat // tm)

    w_in_b, w_fo_b, w_ao_b, w_out_b, w_ff1_b, w_ff2_b = (
        w.astype(BF16) for w in (w_in, w_fo, w_ao, w_out, w_ff1, w_ff2))
    g1 = norm1_g.reshape(depth, 1, D_MODEL)
    g2 = norm2_g.reshape(depth, 1, D_MODEL)
    gf = final_g.reshape(1, D_MODEL)
    rope_tabs = _rope_tables(s_lat)
    dft_ctx = _dft_tables(s_ctx)
    dft_lat = _dft_tables(s_lat)
    cache_k = cache_k.reshape(b_lat, depth, past, KV_WIDTH)
    cache_v = cache_v.reshape(b_lat, depth, past, KV_WIDTH)

    def layer_tail(x, uf, attn, gates, l, row_of_tile, fourier_tabs, batch, seqs_per_step):
        seq = x.shape[0] // batch
        f = _fourier_mix(uf.reshape(batch, seq, FOURIER_WIDTH), fourier_tabs, seqs_per_step)
        return _merge_mlp(x, f.reshape(-1, FOURIER_WIDTH), attn.reshape(-1, ATTN_WIDTH), gates, mod, l, row_of_tile,
                          g2[l], w_fo_b[l], w_ao_b[l], w_out_b[l], w_ff1_b[l], w_ff2_b[l],
                          gf if l == depth - 1 else None, tm)

    xc = x_prompt.reshape(b_ctx * s_ctx, D_MODEL)
    new_k, new_v = [], []
    for l in range(depth):
        uf, q, k, v, gates = _in_projection(xc, mod, l, ctx_row, g1[l], w_in_b[l], None, F32, tm)
        new_k.append(k.reshape(b_ctx, s_ctx, N_KV_HEADS, HEAD_DIM))
        new_v.append(v.reshape(b_ctx, s_ctx, N_KV_HEADS, HEAD_DIM))
        attn = _context_attention(q.reshape(b_ctx, s_ctx, ATTN_WIDTH), k.reshape(b_ctx, s_ctx, KV_WIDTH),
                                  v.reshape(b_ctx, s_ctx, KV_WIDTH), sink[l])
        xc = layer_tail(xc, uf, attn, gates, l, ctx_row, dft_ctx, b_ctx, 4 if b_ctx % 4 == 0 else 1)
    y_prompt = xc.reshape(b_ctx, s_ctx, D_MODEL)

    xs = x_sample.reshape(b_lat * s_lat, D_MODEL)
    for l in range(depth):
        uf, q, k, v, gates = _in_projection(xs, mod, l, lat_row, g1[l], w_in_b[l], rope_tabs, BF16, tm)
        attn = _latent_attention(q.reshape(b_lat, s_lat, ATTN_WIDTH), k.reshape(b_lat, s_lat, KV_WIDTH),
                                 v.reshape(b_lat, s_lat, N_KV_HEADS * LANES), cache_k, cache_v, l, sink[l])
        xs = layer_tail(xs, uf, attn, gates, l, lat_row, dft_lat, b_lat, 1)
    y_sample = xs.reshape(b_lat, s_lat, D_MODEL)

    return y_prompt, y_sample, jnp.stack(new_k, axis=1), jnp.stack(new_v, axis=1)
```

```python
import functools

import numpy as np
import jax
import jax.numpy as jnp
from jax import lax
from jax.experimental import pallas as pl
from jax.experimental.pallas import tpu as pltpu

D_MODEL = 1024
N_HEADS = 16
N_KV_HEADS = 4
HEAD_DIM = 64
Q_PER_KV = N_HEADS // N_KV_HEADS
ATTN_WIDTH = N_HEADS * HEAD_DIM
KV_WIDTH = N_KV_HEADS * HEAD_DIM
N_FOURIER_GROUPS = 4
FOURIER_GROUP_DIM = 128
FOURIER_WIDTH = N_FOURIER_GROUPS * FOURIER_GROUP_DIM
IN_WIDTH = FOURIER_WIDTH + ATTN_WIDTH + 2 * KV_WIDTH + 2 * D_MODEL
D_FF = 4 * D_MODEL
GRID_W = 64
WINDOW = 128
ROPE_THETA = 10000.0
ROPE_PAIRS_PER_AXIS = HEAD_DIM // 4
EPS = 1e-6
NEG_INF = -1e30
LOG2E = 1.4426950408889634
QUERY_SCALE = HEAD_DIM ** -0.5 * LOG2E

OFF_Q = FOURIER_WIDTH
OFF_K = OFF_Q + ATTN_WIDTH
OFF_V = OFF_K + KV_WIDTH
OFF_G = OFF_V + KV_WIDTH

LANES = 128
MOD_ROWS = 16
VMEM_LIMIT_BYTES = 56 * 1024 * 1024
ROW_TILE = 512
COL_CHUNK = 512
Q_BLOCK = 128
KEY_SPAN = 3 * Q_BLOCK
MAX_ROWS = 128
EXP_ROWS = 64
ONES_ROWS = 16
AUG_ROWS = HEAD_DIM + ONES_ROWS

BF16 = jnp.bfloat16
F32 = jnp.float32


def _params(n_axes):
    return pltpu.CompilerParams(dimension_semantics=("arbitrary",) * n_axes,
                                vmem_limit_bytes=VMEM_LIMIT_BYTES)


def _resident(shape, layer=None):
    zeros = (0,) * len(shape)
    if layer is None:
        return pl.BlockSpec(shape, lambda *_: zeros, pipeline_mode=pl.Buffered(1))
    return pl.BlockSpec((None,) + tuple(shape), lambda *_: (layer,) + zeros, pipeline_mode=pl.Buffered(1))


def _dot(a, b):
    return jnp.dot(a, b, preferred_element_type=F32)


def _sigmoid(z):
    return 1.0 / (1.0 + jnp.exp(-z))


def _rmsnorm(x, g):
    return x * lax.rsqrt(jnp.mean(x * x, axis=-1, keepdims=True) + EPS) * g


def _mod_kernel(c_ref, w_ref, b_ref, o_ref):
    c = c_ref[...]
    a = c * _sigmoid(c)
    a_hi = a.astype(BF16)
    a_lo = (a - a_hi.astype(F32)).astype(BF16)
    w = w_ref[...]
    w_hi = w.astype(BF16)
    w_lo = (w - w_hi.astype(F32)).astype(BF16)
    o_ref[...] = _dot(a_hi, w_hi) + _dot(a_lo, w_hi) + _dot(a_hi, w_lo) + b_ref[...]


def _modulation(cvecs, w_ada, b_ada):
    depth, _, width = w_ada.shape
    tn = width // 4
    return pl.pallas_call(
        _mod_kernel,
        grid=(depth, width // tn),
        in_specs=[
            pl.BlockSpec((MOD_ROWS, D_MODEL), lambda l, j: (0, 0)),
            pl.BlockSpec((None, D_MODEL, tn), lambda l, j: (l, 0, j)),
            pl.BlockSpec((None, 1, tn), lambda l, j: (l, 0, j)),
        ],
        out_specs=pl.BlockSpec((None, MOD_ROWS, tn), lambda l, j: (l, 0, j)),
        out_shape=jax.ShapeDtypeStruct((depth, MOD_ROWS, width), F32),
        compiler_params=_params(2),
        name="modulation",
    )(cvecs, w_ada, b_ada.reshape(depth, 1, width))


def _mod_spec(layer, slot, row_of_tile):
    return pl.BlockSpec((None, None, None, 1, D_MODEL),
                        lambda i: (layer, row_of_tile(i), slot, 0, 0))


def _rope(z, cos, sin_even, sin_odd):
    outs = []
    for j in range(z.shape[1] // LANES):
        zj = z[:, j * LANES:(j + 1) * LANES]
        outs.append(zj * cos + pltpu.roll(zj, LANES - 1, 1) * sin_even + pltpu.roll(zj, 1, 1) * sin_odd)
    return outs[0] if len(outs) == 1 else jnp.concatenate(outs, axis=1)


def _augment_values_t(v_t):
    ones = jnp.ones((ONES_ROWS, v_t.shape[1]), v_t.dtype)
    parts = []
    for kv in range(N_KV_HEADS):
        parts += [v_t[kv * HEAD_DIM:(kv + 1) * HEAD_DIM], ones]
    return jnp.concatenate(parts, axis=0).astype(BF16)


def _inproj_kernel(*refs, rope):
    if rope:
        (x_ref, sh_ref, sc_ref, g_ref, w_ref, wvt_ref, cos_ref, se_ref, so_ref,
         uf_ref, q_ref, k_ref, v_ref, gate_ref) = refs
        rot = lambda z: _rope(z, cos_ref[...], se_ref[...], so_ref[...])
    else:
        x_ref, sh_ref, sc_ref, g_ref, w_ref, _, _, uf_ref, q_ref, k_ref, v_ref, gate_ref = refs
        rot = lambda z: z
    h = _rmsnorm(x_ref[...], g_ref[...]) * (1.0 + sc_ref[...]) + sh_ref[...]
    h = h.astype(BF16)

    def seg(start):
        return _dot(h, w_ref[:, start:start + COL_CHUNK])

    uf_ref[...] = seg(0).astype(uf_ref.dtype)
    for j in range(ATTN_WIDTH // COL_CHUNK):
        z = rot(seg(OFF_Q + j * COL_CHUNK)) * QUERY_SCALE
        q_ref[:, j * COL_CHUNK:(j + 1) * COL_CHUNK] = z.astype(q_ref.dtype)
    if rope:
        k_ref[...] = rot(_dot(h, w_ref[:, OFF_K:OFF_V])).astype(k_ref.dtype)
        v_t = lax.dot_general(wvt_ref[...], h, (((1,), (1,)), ((), ())), preferred_element_type=F32)
        v_ref[...] = _augment_values_t(v_t)
    else:
        z = seg(OFF_K)
        k_ref[...] = z[:, :KV_WIDTH].reshape(k_ref.shape)
        v_ref[...] = z[:, KV_WIDTH:].reshape(v_ref.shape)
    for j in range(2 * D_MODEL // COL_CHUNK):
        z = seg(OFF_G + j * COL_CHUNK)
        gate_ref[:, j * COL_CHUNK:(j + 1) * COL_CHUNK] = _sigmoid(z).astype(gate_ref.dtype)


def _in_projection(x, mod, layer, row_of_tile, g1, w_in, tm, *, w_v_t=None, rope_tabs=None, kv_cache=None):
    rows = x.shape[0]
    tile = lambda w: pl.BlockSpec((tm, w), lambda i: (i, 0))
    in_specs = [tile(D_MODEL), _mod_spec(layer, 0, row_of_tile), _mod_spec(layer, 1, row_of_tile),
                _resident((1, D_MODEL), layer), _resident((D_MODEL, IN_WIDTH), layer)]
    args = [x, mod, mod, g1, w_in]
    out_specs = [tile(FOURIER_WIDTH), tile(ATTN_WIDTH), None, None, tile(2 * D_MODEL)]
    out_shape = [jax.ShapeDtypeStruct((rows, FOURIER_WIDTH), BF16), jax.ShapeDtypeStruct((rows, ATTN_WIDTH), BF16),
                 None, None, jax.ShapeDtypeStruct((rows, 2 * D_MODEL), BF16)]
    latent = rope_tabs is not None
    aliases = {}
    if latent:
        tiles_per_seq = rope_tabs[0].shape[0] // tm
        in_specs += [_resident((KV_WIDTH, D_MODEL), layer)]
        in_specs += [pl.BlockSpec((tm, LANES), lambda i: (i % tiles_per_seq, 0))] * 3
        args += [w_v_t] + list(rope_tabs)
        out_specs[2] = tile(KV_WIDTH)
        out_shape[2] = jax.ShapeDtypeStruct((rows, KV_WIDTH), BF16)
        out_specs[3] = pl.BlockSpec((N_KV_HEADS * AUG_ROWS, tm), lambda i: (0, i))
        out_shape[3] = jax.ShapeDtypeStruct((N_KV_HEADS * AUG_ROWS, rows), BF16)
    else:
        s_ctx = kv_cache[0].shape[2]
        assert tm % s_ctx == 0
        in_specs += [pl.BlockSpec(memory_space=pl.ANY)] * 2
        args += list(kv_cache)
        aliases = {5: 2, 6: 3}
        for slot in (2, 3):
            out_specs[slot] = pl.BlockSpec((tm // s_ctx, None, s_ctx, KV_WIDTH), lambda i: (i, layer, 0, 0))
            out_shape[slot] = jax.ShapeDtypeStruct(kv_cache[0].shape, F32)
    return pl.pallas_call(
        functools.partial(_inproj_kernel, rope=latent),
        grid=(rows // tm,),
        in_specs=in_specs,
        out_specs=out_specs,
        out_shape=out_shape,
        input_output_aliases=aliases,
        compiler_params=_params(1),
        name="in_projection",
    )(*args)


def _softmax_head(s_ref, p_ref, cols, bias_t_of, n_biased, sink):
    n_keys = s_ref.shape[0]
    blocks = []
    for r in range(0, n_keys, MAX_ROWS):
        rows = slice(r, r + MAX_ROWS)
        blk = s_ref[rows, cols]
        blocks.append(blk + bias_t_of(rows) if r < n_biased else blk)
    m = jnp.maximum(jnp.max(functools.reduce(jnp.maximum, blocks), axis=0, keepdims=True), sink)
    for r in range(0, n_keys, EXP_ROWS):
        rows = slice(r, r + EXP_ROWS)
        z = s_ref[rows, cols] - m
        if r < n_biased:
            z = z + bias_t_of(rows)
        p_ref[rows, cols] = jnp.exp2(z).astype(BF16)
    return jnp.exp2(sink - m)


def _attend(q_ref, k_of, v_t_of, bias_t_of, n_biased, sink_of, s_ref, p_ref, o_ref):
    t = q_ref.shape[0]

    def score(kv):
        heads = [q_ref[:, (kv * Q_PER_KV + g) * HEAD_DIM:(kv * Q_PER_KV + g + 1) * HEAD_DIM] for g in range(Q_PER_KV)]
        s_ref[kv] = lax.dot_general(k_of(kv), jnp.concatenate(heads, axis=0), (((1,), (1,)), ((), ())),
                                    preferred_element_type=F32)

    score(0)
    for kv in range(N_KV_HEADS):
        if kv + 1 < N_KV_HEADS:
            score(kv + 1)
        sink_terms = []
        for g in range(Q_PER_KV):
            sink = sink_of(kv * Q_PER_KV + g) * LOG2E
            sink_terms.append(_softmax_head(s_ref.at[kv], p_ref.at[kv], slice(g * t, (g + 1) * t),
                                            bias_t_of, n_biased, sink))
        o_aug = _dot(v_t_of(kv), p_ref[kv])
        denom = o_aug[HEAD_DIM:HEAD_DIM + 1, :] + jnp.concatenate(sink_terms, axis=1)
        o_t = o_aug[:HEAD_DIM, :] / denom
        for pair in range(Q_PER_KV // 2):
            two = jnp.concatenate([o_t[:, (2 * pair) * t:(2 * pair + 1) * t],
                                   o_t[:, (2 * pair + 1) * t:(2 * pair + 2) * t]], axis=0)
            col = (kv * (Q_PER_KV // 2) + pair) * LANES
            o_ref[:, col:col + LANES] = two.T.astype(o_ref.dtype)


def _attn_ctx_kernel(sink_ref, q_ref, k_ref, v_ref, o_ref, s_ref, p_ref, *, layer):
    k = k_ref[...].astype(BF16)
    v_t = _augment_values_t(v_ref[...].T)
    _attend(q_ref, lambda kv: k[:, kv * HEAD_DIM:(kv + 1) * HEAD_DIM],
            lambda kv: v_t[kv * AUG_ROWS:(kv + 1) * AUG_ROWS], None, 0,
            lambda h: sink_ref[layer, h], s_ref, p_ref, o_ref)


def _context_attention(q, k, v, sink, layer):
    b, s, _ = q.shape
    assert s % MAX_ROWS == 0
    blk = pl.BlockSpec((None, s, ATTN_WIDTH), lambda i: (i, 0, 0))
    cache = pl.BlockSpec((None, None, s, KV_WIDTH), lambda i: (i, layer, 0, 0))
    return pl.pallas_call(
        functools.partial(_attn_ctx_kernel, layer=layer),
        grid=(b,),
        in_specs=[pl.BlockSpec(memory_space=pltpu.SMEM), blk, cache, cache],
        out_specs=blk,
        out_shape=jax.ShapeDtypeStruct((b, s, ATTN_WIDTH), BF16),
        scratch_shapes=[pltpu.VMEM((N_KV_HEADS, s, Q_PER_KV * s), F32),
                        pltpu.VMEM((N_KV_HEADS, s, Q_PER_KV * s), BF16)],
        compiler_params=_params(1),
        name="context_attention",
    )(sink, q, k, v)


def _window_bias_t():
    c = np.arange(KEY_SPAN)[None, :, None]
    r = np.arange(Q_BLOCK)[None, None, :]
    p = np.arange(KEY_SPAN // Q_BLOCK)[:, None, None]
    return jnp.asarray(np.where(np.abs(c - (r + p * Q_BLOCK)) <= WINDOW, 0.0, NEG_INF).astype(np.float32))


def _attn_lat_kernel(sink_ref, q_ref, k_ref, vt_ref, kc_ref, vc_ref, bias_ref, o_ref, kall_ref, vall_ref, s_ref, p_ref,
                     *, layer):
    t = q_ref.shape[0]
    s_len = k_ref.shape[0]
    i = pl.program_id(1)

    @pl.when(i == 0)
    def _():
        kall_ref[KEY_SPAN:, :] = kc_ref[...].astype(BF16)
        vall_ref[:, KEY_SPAN:] = _augment_values_t(vc_ref[...].T)

    start = pl.multiple_of(jnp.clip((i - 1) * t, 0, s_len - KEY_SPAN), t)
    kall_ref[:KEY_SPAN, :] = k_ref[pl.ds(start, KEY_SPAN), :]
    vall_ref[:, :KEY_SPAN] = vt_ref[:, pl.ds(start, KEY_SPAN)]
    pattern = lax.div(i * t - start, t)
    _attend(q_ref, lambda kv: kall_ref[:, kv * HEAD_DIM:(kv + 1) * HEAD_DIM],
            lambda kv: vall_ref[kv * AUG_ROWS:(kv + 1) * AUG_ROWS, :],
            lambda rows: bias_ref[pattern, rows, :], KEY_SPAN,
            lambda h: sink_ref[layer, h], s_ref, p_ref, o_ref)


def _latent_attention(q, k, v_t, cache_k, cache_v, sink, layer):
    b, s, _ = q.shape
    past = cache_k.shape[2]
    assert s % Q_BLOCK == 0 and s >= KEY_SPAN and WINDOW <= Q_BLOCK and Q_BLOCK == LANES and past % MAX_ROWS == 0
    n_keys = KEY_SPAN + past
    qblk = pl.BlockSpec((None, Q_BLOCK, ATTN_WIDTH), lambda bi, i: (bi, i, 0))
    cache = pl.BlockSpec((None, None, past, KV_WIDTH), lambda bi, i: (bi, layer, 0, 0))
    bias = _window_bias_t()
    return pl.pallas_call(
        functools.partial(_attn_lat_kernel, layer=layer),
        grid=(b, s // Q_BLOCK),
        in_specs=[pl.BlockSpec(memory_space=pltpu.SMEM), qblk,
                  pl.BlockSpec((None, s, KV_WIDTH), lambda bi, i: (bi, 0, 0)),
                  pl.BlockSpec((N_KV_HEADS * AUG_ROWS, s), lambda bi, i: (0, bi)),
                  cache, cache, _resident(bias.shape)],
        out_specs=qblk,
        out_shape=jax.ShapeDtypeStruct((b, s, ATTN_WIDTH), BF16),
        scratch_shapes=[pltpu.VMEM((n_keys, KV_WIDTH), BF16), pltpu.VMEM((N_KV_HEADS * AUG_ROWS, n_keys), BF16),
                        pltpu.VMEM((N_KV_HEADS, n_keys, Q_PER_KV * Q_BLOCK), F32),
                        pltpu.VMEM((N_KV_HEADS, n_keys, Q_PER_KV * Q_BLOCK), BF16)],
        compiler_params=_params(2),
        name="latent_attention",
    )(sink, q, k, v_t, cache_k, cache_v, bias)


def _dft_tables(s):
    def cos_sin(n):
        idx = np.arange(n, dtype=np.int64)
        ang = 2.0 * np.pi * ((idx[:, None] * idx[None, :]) % n) / n
        scale = n ** -0.5
        return (np.cos(ang) * scale).astype(np.float32), (np.sin(ang) * scale).astype(np.float32)
    cc, sc = cos_sin(FOURIER_GROUP_DIM)
    cs, ss = cos_sin(s)
    chan = jnp.asarray(np.concatenate([cc, sc], axis=1)).astype(BF16)
    pos = jnp.asarray(np.concatenate([cs, -ss], axis=1)).astype(BF16)
    return chan, pos


def _fourier_kernel(x_ref, chan_ref, pos_ref, o_ref, ab_ref, *, row_chunk):
    nb, s, _ = x_ref.shape
    gd = FOURIER_GROUP_DIM
    for b in range(nb):
        for r in range(s // row_chunk):
            rs = slice(r * row_chunk, (r + 1) * row_chunk)
            for g in range(N_FOURIER_GROUPS):
                ab = _dot(x_ref[b, rs, g * gd:(g + 1) * gd], chan_ref[...])
                ab_ref[rs, g * gd:(g + 1) * gd] = ab[:, :gd].astype(BF16)
                ab_ref[s + r * row_chunk: s + (r + 1) * row_chunk, g * gd:(g + 1) * gd] = ab[:, gd:].astype(BF16)
        for r in range(s // row_chunk):
            rs = slice(r * row_chunk, (r + 1) * row_chunk)
            o_ref[b, rs, :] = _dot(pos_ref[rs, :], ab_ref[...]).astype(o_ref.dtype)


def _fourier_mix(u, tables, seqs_per_step):
    b, s, w = u.shape
    chan, pos = tables
    row_chunk = min(s, 512)
    blk = pl.BlockSpec((seqs_per_step, s, w), lambda i: (i, 0, 0))
    return pl.pallas_call(
        functools.partial(_fourier_kernel, row_chunk=row_chunk),
        grid=(b // seqs_per_step,),
        in_specs=[blk, _resident(chan.shape), _resident(pos.shape)],
        out_specs=blk,
        out_shape=jax.ShapeDtypeStruct((b, s, w), BF16),
        scratch_shapes=[pltpu.VMEM((2 * s, w), BF16)],
        compiler_params=_params(1),
        name="fourier_mix",
    )(u, chan, pos)


def _merge_mlp_kernel(*refs, final):
    (x_ref, f_ref, a_ref, gate_ref, ga1_ref, sh2_ref, sc2_ref, ga2_ref, g2_ref,
     wfo_ref, wao_ref, wout_ref, w1_ref, w2_ref) = refs[:14]
    o_ref = refs[-1]
    gate_f = gate_ref[:, :D_MODEL].astype(F32)
    gate_a = gate_ref[:, D_MODEL:].astype(F32)
    m = gate_f * _dot(f_ref[...], wfo_ref[...]) + gate_a * _dot(a_ref[...], wao_ref[...])
    x1 = x_ref[...] + ga1_ref[...] * _dot(m.astype(BF16), wout_ref[...])
    h = (_rmsnorm(x1, g2_ref[...]) * (1.0 + sc2_ref[...]) + sh2_ref[...]).astype(BF16)
    ff = None
    for j in range(D_FF // COL_CHUNK):
        cs = slice(j * COL_CHUNK, (j + 1) * COL_CHUNK)
        t = jnp.maximum(_dot(h, w1_ref[:, cs]), 0.0)
        part = _dot((t * t).astype(BF16), w2_ref[cs, :])
        ff = part if ff is None else ff + part
    x2 = x1 + ga2_ref[...] * ff
    if final:
        x2 = _rmsnorm(x2, refs[14][...])
    o_ref[...] = x2


def _merge_mlp(x, f, attn, gates, mod, layer, row_of_tile, g2, w_fo, w_ao, w_out, w_ff1, w_ff2, final_g, tm):
    rows = x.shape[0]
    tile = lambda w: pl.BlockSpec((tm, w), lambda i: (i, 0))
    in_specs = [tile(D_MODEL), tile(FOURIER_WIDTH), tile(ATTN_WIDTH), tile(2 * D_MODEL)]
    in_specs += [_mod_spec(layer, slot, row_of_tile) for slot in (2, 3, 4, 5)]
    in_specs += [_resident((1, D_MODEL), layer)]
    in_specs += [_resident(w.shape[1:], layer) for w in (w_fo, w_ao, w_out, w_ff1, w_ff2)]
    args = [x, f, attn, gates, mod, mod, mod, mod, g2, w_fo, w_ao, w_out, w_ff1, w_ff2]
    if final_g is not None:
        in_specs.append(_resident((1, D_MODEL)))
        args.append(final_g)
    return pl.pallas_call(
        functools.partial(_merge_mlp_kernel, final=final_g is not None),
        grid=(rows // tm,),
        in_specs=in_specs,
        out_specs=tile(D_MODEL),
        out_shape=jax.ShapeDtypeStruct((rows, D_MODEL), F32),
        compiler_params=_params(1),
        name="merge_mlp",
    )(*args)


def _rope_tables(n_tokens):
    t = jnp.arange(n_tokens)
    row = (t // GRID_W).astype(F32)
    col = (t % GRID_W).astype(F32)
    inv = ROPE_THETA ** (-jnp.arange(ROPE_PAIRS_PER_AXIS, dtype=F32) / ROPE_PAIRS_PER_AXIS)
    ang = jnp.concatenate([row[:, None] * inv[None, :], col[:, None] * inv[None, :]], axis=-1)
    cos = jnp.repeat(jnp.cos(ang), 2, axis=1)
    sin = jnp.repeat(jnp.sin(ang), 2, axis=1)
    even = (jnp.arange(HEAD_DIM) % 2 == 0)[None, :]
    reps = LANES // HEAD_DIM
    return (jnp.tile(cos, (1, reps)),
            jnp.tile(jnp.where(even, -sin, 0.0), (1, reps)),
            jnp.tile(jnp.where(even, 0.0, sin), (1, reps)))


def kernel(x_prompt, x_sample, c, cache_k, cache_v, c_ctx, w_ada, b_ada, norm1_g, norm2_g,
           w_in, sink, w_fo, w_ao, w_out, w_ff1, w_ff2, final_g):
    depth = w_in.shape[0]
    b_ctx, s_ctx, _ = x_prompt.shape
    b_lat, s_lat, _ = x_sample.shape
    past = cache_k.shape[2]
    assert 1 + b_lat <= MOD_ROWS
    tm = ROW_TILE
    assert (b_ctx * s_ctx) % tm == 0 and s_lat % tm == 0

    cvecs = jnp.zeros((MOD_ROWS, D_MODEL), F32).at[0].set(c_ctx).at[1:1 + b_lat].set(c)
    mod = _modulation(cvecs, w_ada, b_ada).reshape(depth, MOD_ROWS, 6, 1, D_MODEL)
    ctx_row = lambda i: 0
    lat_row = lambda i: 1 + i // (s_lat // tm)

    w_in_b, w_fo_b, w_ao_b, w_out_b, w_ff1_b, w_ff2_b = (
        w.astype(BF16) for w in (w_in, w_fo, w_ao, w_out, w_ff1, w_ff2))
    w_v_t = jnp.swapaxes(w_in_b[:, :, OFF_V:OFF_G], 1, 2)
    g1 = norm1_g.reshape(depth, 1, D_MODEL)
    g2 = norm2_g.reshape(depth, 1, D_MODEL)
    gf = final_g.reshape(1, D_MODEL)
    rope_tabs = _rope_tables(s_lat)
    dft_ctx = _dft_tables(s_ctx)
    dft_lat = _dft_tables(s_lat)
    cache_k = cache_k.reshape(b_lat, depth, past, KV_WIDTH)
    cache_v = cache_v.reshape(b_lat, depth, past, KV_WIDTH)

    def layer_tail(x, uf, attn, gates, l, row_of_tile, fourier_tabs, batch, seqs_per_step):
        seq = x.shape[0] // batch
        f = _fourier_mix(uf.reshape(batch, seq, FOURIER_WIDTH), fourier_tabs, seqs_per_step)
        return _merge_mlp(x, f.reshape(-1, FOURIER_WIDTH), attn.reshape(-1, ATTN_WIDTH), gates, mod, l, row_of_tile,
                          g2, w_fo_b, w_ao_b, w_out_b, w_ff1_b, w_ff2_b, gf if l == depth - 1 else None, tm)

    xc = x_prompt.reshape(b_ctx * s_ctx, D_MODEL)
    kv_cache = (jnp.zeros((b_ctx, depth, s_ctx, KV_WIDTH), F32),) * 2
    for l in range(depth):
        uf, q, *kv_cache, gates = _in_projection(xc, mod, l, ctx_row, g1, w_in_b, tm, kv_cache=kv_cache)
        attn = _context_attention(q.reshape(b_ctx, s_ctx, ATTN_WIDTH), *kv_cache, sink, l)
        xc = layer_tail(xc, uf, attn, gates, l, ctx_row, dft_ctx, b_ctx, 4 if b_ctx % 4 == 0 else 1)
    y_prompt = xc.reshape(b_ctx, s_ctx, D_MODEL)
    new_k, new_v = (buf.reshape(b_ctx, depth, s_ctx, N_KV_HEADS, HEAD_DIM) for buf in kv_cache)

    xs = x_sample.reshape(b_lat * s_lat, D_MODEL)
    for l in range(depth):
        uf, q, k, v_t, gates = _in_projection(xs, mod, l, lat_row, g1, w_in_b, tm, w_v_t=w_v_t, rope_tabs=rope_tabs)
        attn = _latent_attention(q.reshape(b_lat, s_lat, ATTN_WIDTH), k.reshape(b_lat, s_lat, KV_WIDTH),
                                 v_t, cache_k, cache_v, sink, l)
        xs = layer_tail(xs, uf, attn, gates, l, lat_row, dft_lat, b_lat, 1)
    y_sample = xs.reshape(b_lat, s_lat, D_MODEL)

    return y_prompt, y_sample, new_k, new_v
```

```python
import functools

import numpy as np
import jax
import jax.numpy as jnp
from jax import lax
from jax.experimental import pallas as pl
from jax.experimental.pallas import tpu as pltpu

D_MODEL = 1024
N_HEADS = 16
N_KV_HEADS = 4
HEAD_DIM = 64
Q_PER_KV = N_HEADS // N_KV_HEADS
ATTN_WIDTH = N_HEADS * HEAD_DIM
KV_WIDTH = N_KV_HEADS * HEAD_DIM
N_FOURIER_GROUPS = 4
FOURIER_GROUP_DIM = 128
FOURIER_WIDTH = N_FOURIER_GROUPS * FOURIER_GROUP_DIM
IN_WIDTH = FOURIER_WIDTH + ATTN_WIDTH + 2 * KV_WIDTH + 2 * D_MODEL
D_FF = 4 * D_MODEL
GRID_W = 64
WINDOW = 128
ROPE_THETA = 10000.0
ROPE_PAIRS_PER_AXIS = HEAD_DIM // 4
EPS = 1e-6
NEG_INF = -1e30
LOG2E = 1.4426950408889634
QUERY_SCALE = HEAD_DIM ** -0.5 * LOG2E

OFF_Q = FOURIER_WIDTH
OFF_K = OFF_Q + ATTN_WIDTH
OFF_V = OFF_K + KV_WIDTH
OFF_G = OFF_V + KV_WIDTH

LANES = 128
MOD_ROWS = 16
VMEM_LIMIT_BYTES = 56 * 1024 * 1024
ROW_TILE = 512
INPROJ_ROW_TILE = 1024
SUB_TILE = 512
COL_CHUNK = 512
Q_BLOCK = 128
KEY_SPAN = 3 * Q_BLOCK
MAX_ROWS = 128
EXP_ROWS = 64
LATENT_QBLOCKS_PER_STEP = 4
CTX_SEQS_PER_STEP = 4
ONES_ROWS = 16
AUG_ROWS = HEAD_DIM + ONES_ROWS

BF16 = jnp.bfloat16
F32 = jnp.float32


def _params(n_axes):
    return pltpu.CompilerParams(dimension_semantics=("arbitrary",) * n_axes,
                                vmem_limit_bytes=VMEM_LIMIT_BYTES)


def _resident(shape, layer=None):
    zeros = (0,) * len(shape)
    if layer is None:
        return pl.BlockSpec(shape, lambda *_: zeros, pipeline_mode=pl.Buffered(1))
    return pl.BlockSpec((None,) + tuple(shape), lambda *_: (layer,) + zeros, pipeline_mode=pl.Buffered(1))


def _dot(a, b):
    return jnp.dot(a, b, preferred_element_type=F32)


def _sigmoid(z):
    return 1.0 / (1.0 + jnp.exp(-z))


def _rmsnorm(x, g):
    return x * lax.rsqrt(jnp.mean(x * x, axis=-1, keepdims=True) + EPS) * g


def _mod_kernel(c_ref, w_ref, b_ref, o_ref):
    c = c_ref[...]
    a = c * _sigmoid(c)
    a_hi = a.astype(BF16)
    a_lo = (a - a_hi.astype(F32)).astype(BF16)
    w = w_ref[...]
    w_hi = w.astype(BF16)
    w_lo = (w - w_hi.astype(F32)).astype(BF16)
    o_ref[...] = _dot(a_hi, w_hi) + _dot(a_lo, w_hi) + _dot(a_hi, w_lo) + b_ref[...]


def _modulation(cvecs, w_ada, b_ada):
    depth, _, width = w_ada.shape
    tn = width // 4
    return pl.pallas_call(
        _mod_kernel,
        grid=(depth, width // tn),
        in_specs=[
            pl.BlockSpec((MOD_ROWS, D_MODEL), lambda l, j: (0, 0)),
            pl.BlockSpec((None, D_MODEL, tn), lambda l, j: (l, 0, j)),
            pl.BlockSpec((None, 1, tn), lambda l, j: (l, 0, j)),
        ],
        out_specs=pl.BlockSpec((None, MOD_ROWS, tn), lambda l, j: (l, 0, j)),
        out_shape=jax.ShapeDtypeStruct((depth, MOD_ROWS, width), F32),
        compiler_params=_params(2),
        name="modulation",
    )(cvecs, w_ada, b_ada.reshape(depth, 1, width))


def _mod_spec(layer, slot, row_of_tile):
    return pl.BlockSpec((None, None, None, 1, D_MODEL),
                        lambda i: (layer, row_of_tile(i), slot, 0, 0))


def _rope(z, cos, sin_even, sin_odd):
    outs = []
    for j in range(z.shape[1] // LANES):
        zj = z[:, j * LANES:(j + 1) * LANES]
        outs.append(zj * cos + pltpu.roll(zj, LANES - 1, 1) * sin_even + pltpu.roll(zj, 1, 1) * sin_odd)
    return outs[0] if len(outs) == 1 else jnp.concatenate(outs, axis=1)


def _augment_values_t(v_t):
    ones = jnp.ones((ONES_ROWS, v_t.shape[1]), v_t.dtype)
    parts = []
    for kv in range(N_KV_HEADS):
        parts += [v_t[kv * HEAD_DIM:(kv + 1) * HEAD_DIM], ones]
    return jnp.concatenate(parts, axis=0).astype(BF16)


def _inproj_kernel(*refs, rope):
    if rope:
        (x_ref, sh_ref, sc_ref, g_ref, w_ref, wvt_ref, cos_ref, se_ref, so_ref,
         uf_ref, q_ref, k_ref, v_ref, gate_ref) = refs
        rot = lambda z, rows: _rope(z, cos_ref[rows, :], se_ref[rows, :], so_ref[rows, :])
    else:
        x_ref, sh_ref, sc_ref, g_ref, w_ref, _, _, uf_ref, q_ref, k_ref, v_ref, gate_ref = refs
        rot = lambda z, rows: z
    for r0 in range(0, x_ref.shape[0], SUB_TILE):
        rows = slice(r0, r0 + SUB_TILE)
        h = _rmsnorm(x_ref[rows, :], g_ref[...]) * (1.0 + sc_ref[...]) + sh_ref[...]
        h = h.astype(BF16)

        def seg(start, h=h):
            return _dot(h, w_ref[:, start:start + COL_CHUNK])

        uf_ref[rows, :] = seg(0).astype(uf_ref.dtype)
        for j in range(ATTN_WIDTH // COL_CHUNK):
            z = rot(seg(OFF_Q + j * COL_CHUNK), rows) * QUERY_SCALE
            q_ref[rows, j * COL_CHUNK:(j + 1) * COL_CHUNK] = z.astype(q_ref.dtype)
        if rope:
            k_ref[rows, :] = rot(_dot(h, w_ref[:, OFF_K:OFF_V]), rows).astype(k_ref.dtype)
            v_t = lax.dot_general(wvt_ref[...], h, (((1,), (1,)), ((), ())), preferred_element_type=F32)
            v_ref[:, rows] = _augment_values_t(v_t)
        else:
            z = seg(OFF_K)
            seqs = slice(r0 // k_ref.shape[1], (r0 + SUB_TILE) // k_ref.shape[1])
            k_ref[seqs] = z[:, :KV_WIDTH].reshape(k_ref[seqs].shape)
            v_ref[seqs] = z[:, KV_WIDTH:].reshape(v_ref[seqs].shape)
        for j in range(2 * D_MODEL // COL_CHUNK):
            z = seg(OFF_G + j * COL_CHUNK)
            gate_ref[rows, j * COL_CHUNK:(j + 1) * COL_CHUNK] = _sigmoid(z).astype(gate_ref.dtype)


def _in_projection(x, mod, layer, row_of_tile, g1, w_in, tm, *, w_v_t=None, rope_tabs=None, kv_cache=None):
    rows = x.shape[0]
    tile = lambda w: pl.BlockSpec((tm, w), lambda i: (i, 0))
    in_specs = [tile(D_MODEL), _mod_spec(layer, 0, row_of_tile), _mod_spec(layer, 1, row_of_tile),
                _resident((1, D_MODEL), layer), _resident((D_MODEL, IN_WIDTH), layer)]
    args = [x, mod, mod, g1, w_in]
    out_specs = [tile(FOURIER_WIDTH), tile(ATTN_WIDTH), None, None, tile(2 * D_MODEL)]
    out_shape = [jax.ShapeDtypeStruct((rows, FOURIER_WIDTH), BF16), jax.ShapeDtypeStruct((rows, ATTN_WIDTH), BF16),
                 None, None, jax.ShapeDtypeStruct((rows, 2 * D_MODEL), BF16)]
    latent = rope_tabs is not None
    aliases = {}
    if latent:
        tiles_per_seq = rope_tabs[0].shape[0] // tm
        in_specs += [_resident((KV_WIDTH, D_MODEL), layer)]
        in_specs += [pl.BlockSpec((tm, LANES), lambda i: (i % tiles_per_seq, 0))] * 3
        args += [w_v_t] + list(rope_tabs)
        out_specs[2] = tile(KV_WIDTH)
        out_shape[2] = jax.ShapeDtypeStruct((rows, KV_WIDTH), BF16)
        out_specs[3] = pl.BlockSpec((N_KV_HEADS * AUG_ROWS, tm), lambda i: (0, i))
        out_shape[3] = jax.ShapeDtypeStruct((N_KV_HEADS * AUG_ROWS, rows), BF16)
    else:
        s_ctx = kv_cache[0].shape[2]
        assert tm % s_ctx == 0
        in_specs += [pl.BlockSpec(memory_space=pl.ANY)] * 2
        args += list(kv_cache)
        aliases = {5: 2, 6: 3}
        for slot in (2, 3):
            out_specs[slot] = pl.BlockSpec((tm // s_ctx, None, s_ctx, KV_WIDTH), lambda i: (i, layer, 0, 0))
            out_shape[slot] = jax.ShapeDtypeStruct(kv_cache[0].shape, F32)
    return pl.pallas_call(
        functools.partial(_inproj_kernel, rope=latent),
        grid=(rows // tm,),
        in_specs=in_specs,
        out_specs=out_specs,
        out_shape=out_shape,
        input_output_aliases=aliases,
        compiler_params=_params(1),
        name="in_projection",
    )(*args)


def _softmax_head(s_ref, p_ref, cols, bias_t_of, n_biased, sink):
    n_keys = s_ref.shape[0]
    blocks = []
    for r in range(0, n_keys, MAX_ROWS):
        rows = slice(r, r + MAX_ROWS)
        blk = s_ref[rows, cols]
        blocks.append(blk + bias_t_of(rows) if r < n_biased else blk)
    m = jnp.maximum(jnp.max(functools.reduce(jnp.maximum, blocks), axis=0, keepdims=True), sink)
    for r in range(0, n_keys, EXP_ROWS):
        rows = slice(r, r + EXP_ROWS)
        z = s_ref[rows, cols] - m
        if r < n_biased:
            z = z + bias_t_of(rows)
        p_ref[rows, cols] = jnp.exp2(z).astype(BF16)
    return jnp.exp2(sink - m)


def _attend(q_of, key_parts, bias_t_of, n_biased, sink_of, s_ref, p_ref, o_ref):
    t = o_ref.shape[0]

    def score(kv):
        qs = jnp.concatenate([q_of(kv * Q_PER_KV + g) for g in range(Q_PER_KV)], axis=0)
        row = 0
        for k_of, _ in key_parts:
            k = k_of(kv)
            s_ref[kv, row:row + k.shape[0], :] = lax.dot_general(k, qs, (((1,), (1,)), ((), ())),
                                                                 preferred_element_type=F32)
            row += k.shape[0]

    score(0)
    for kv in range(N_KV_HEADS):
        if kv + 1 < N_KV_HEADS:
            score(kv + 1)
        sink_terms = []
        for g in range(Q_PER_KV):
            sink = sink_of(kv * Q_PER_KV + g) * LOG2E
            sink_terms.append(_softmax_head(s_ref.at[kv], p_ref.at[kv], slice(g * t, (g + 1) * t),
                                            bias_t_of, n_biased, sink))
        o_aug, row = None, 0
        for _, v_t_of in key_parts:
            v_t = v_t_of(kv)
            part = _dot(v_t, p_ref[kv, row:row + v_t.shape[1], :])
            o_aug = part if o_aug is None else o_aug + part
            row += v_t.shape[1]
        denom = o_aug[HEAD_DIM:HEAD_DIM + 1, :] + jnp.concatenate(sink_terms, axis=1)
        o_t = o_aug[:HEAD_DIM, :] / denom
        for pair in range(Q_PER_KV // 2):
            two = jnp.concatenate([o_t[:, (2 * pair) * t:(2 * pair + 1) * t],
                                   o_t[:, (2 * pair + 1) * t:(2 * pair + 2) * t]], axis=0)
            col = (kv * (Q_PER_KV // 2) + pair) * LANES
            o_ref[:, col:col + LANES] = two.T.astype(o_ref.dtype)


def _head_cols(head):
    return slice(head * HEAD_DIM, (head + 1) * HEAD_DIM)


def _attn_ctx_kernel(sink_ref, q_ref, k_ref, v_ref, o_ref, s_ref, p_ref, *, layer):
    for b in range(q_ref.shape[0]):
        k = k_ref[b].astype(BF16)
        v_t = _augment_values_t(v_ref[b].T)
        _attend(lambda h: q_ref[b, :, _head_cols(h)],
                [(lambda kv: k[:, _head_cols(kv)], lambda kv: v_t[kv * AUG_ROWS:(kv + 1) * AUG_ROWS])],
                None, 0, lambda h: sink_ref[layer, h], s_ref.at[b], p_ref.at[b], o_ref.at[b])


def _context_attention(q, k, v, sink, layer):
    b, s, _ = q.shape
    nb = CTX_SEQS_PER_STEP
    assert s % MAX_ROWS == 0 and b % nb == 0
    blk = pl.BlockSpec((nb, s, ATTN_WIDTH), lambda i: (i, 0, 0))
    cache = pl.BlockSpec((nb, None, s, KV_WIDTH), lambda i: (i, layer, 0, 0))
    return pl.pallas_call(
        functools.partial(_attn_ctx_kernel, layer=layer),
        grid=(b // nb,),
        in_specs=[pl.BlockSpec(memory_space=pltpu.SMEM), blk, cache, cache],
        out_specs=blk,
        out_shape=jax.ShapeDtypeStruct((b, s, ATTN_WIDTH), BF16),
        scratch_shapes=[pltpu.VMEM((nb, N_KV_HEADS, s, Q_PER_KV * s), F32),
                        pltpu.VMEM((nb, N_KV_HEADS, s, Q_PER_KV * s), BF16)],
        compiler_params=_params(1),
        name="context_attention",
    )(sink, q, k, v)


def _window_bias_t():
    c = np.arange(KEY_SPAN)[None, :, None]
    r = np.arange(Q_BLOCK)[None, None, :]
    p = np.arange(KEY_SPAN // Q_BLOCK)[:, None, None]
    return jnp.asarray(np.where(np.abs(c - (r + p * Q_BLOCK)) <= WINDOW, 0.0, NEG_INF).astype(np.float32))


def _attn_lat_kernel(sink_ref, q_ref, k_ref, vt_ref, kc_ref, vc_ref, bias_ref, o_ref, kcb_ref, vcb_ref, s_ref, p_ref,
                     *, layer):
    t = Q_BLOCK
    s_len = k_ref.shape[0]

    @pl.when(pl.program_id(1) == 0)
    def _():
        kcb_ref[...] = kc_ref[...].astype(BF16)
        vcb_ref[...] = _augment_values_t(vc_ref[...].T)

    context = (lambda kv: kcb_ref[:, _head_cols(kv)], lambda kv: vcb_ref[kv * AUG_ROWS:(kv + 1) * AUG_ROWS, :])
    for j in range(q_ref.shape[0] // t):
        i = pl.program_id(1) * (q_ref.shape[0] // t) + j
        start = pl.multiple_of(jnp.clip((i - 1) * t, 0, s_len - KEY_SPAN), t)
        pattern = lax.div(i * t - start, t)
        window = (lambda kv, start=start: k_ref[pl.ds(start, KEY_SPAN), _head_cols(kv)],
                  lambda kv, start=start: vt_ref[kv * AUG_ROWS:(kv + 1) * AUG_ROWS, pl.ds(start, KEY_SPAN)])
        qrows = slice(j * t, (j + 1) * t)
        _attend(lambda h, qrows=qrows: q_ref[qrows, _head_cols(h)], [window, context],
                lambda rows, pattern=pattern: bias_ref[pattern, rows, :], KEY_SPAN,
                lambda h: sink_ref[layer, h], s_ref.at[j], p_ref.at[j], o_ref.at[qrows])


def _latent_attention(q, k, v_t, cache_k, cache_v, sink, layer):
    b, s, _ = q.shape
    past = cache_k.shape[2]
    nq = LATENT_QBLOCKS_PER_STEP
    assert s % (nq * Q_BLOCK) == 0 and s >= KEY_SPAN and WINDOW <= Q_BLOCK and Q_BLOCK == LANES
    assert past % MAX_ROWS == 0
    n_keys = KEY_SPAN + past
    qblk = pl.BlockSpec((None, nq * Q_BLOCK, ATTN_WIDTH), lambda bi, i: (bi, i, 0))
    cache = pl.BlockSpec((None, None, past, KV_WIDTH), lambda bi, i: (bi, layer, 0, 0))
    bias = _window_bias_t()
    return pl.pallas_call(
        functools.partial(_attn_lat_kernel, layer=layer),
        grid=(b, s // (nq * Q_BLOCK)),
        in_specs=[pl.BlockSpec(memory_space=pltpu.SMEM), qblk,
                  pl.BlockSpec((None, s, KV_WIDTH), lambda bi, i: (bi, 0, 0)),
                  pl.BlockSpec((N_KV_HEADS * AUG_ROWS, s), lambda bi, i: (0, bi)),
                  cache, cache, _resident(bias.shape)],
        out_specs=qblk,
        out_shape=jax.ShapeDtypeStruct((b, s, ATTN_WIDTH), BF16),
        scratch_shapes=[pltpu.VMEM((past, KV_WIDTH), BF16), pltpu.VMEM((N_KV_HEADS * AUG_ROWS, past), BF16),
                        pltpu.VMEM((nq, N_KV_HEADS, n_keys, Q_PER_KV * Q_BLOCK), F32),
                        pltpu.VMEM((nq, N_KV_HEADS, n_keys, Q_PER_KV * Q_BLOCK), BF16)],
        compiler_params=_params(2),
        name="latent_attention",
    )(sink, q, k, v_t, cache_k, cache_v, bias)


def _dft_tables(s):
    def cos_sin(n):
        idx = np.arange(n, dtype=np.int64)
        ang = 2.0 * np.pi * ((idx[:, None] * idx[None, :]) % n) / n
        scale = n ** -0.5
        return (np.cos(ang) * scale).astype(np.float32), (np.sin(ang) * scale).astype(np.float32)
    cc, sc = cos_sin(FOURIER_GROUP_DIM)
    cs, ss = cos_sin(s)
    chan = jnp.asarray(np.concatenate([cc, sc], axis=1)).astype(BF16)
    pos = jnp.asarray(np.concatenate([cs, -ss], axis=1)).astype(BF16)
    return chan, pos


def _fourier_kernel(x_ref, chan_ref, pos_ref, o_ref, ab_ref, *, row_chunk):
    nb, s, _ = x_ref.shape
    gd = FOURIER_GROUP_DIM
    for b in range(nb):
        for r in range(s // row_chunk):
            rs = slice(r * row_chunk, (r + 1) * row_chunk)
            for g in range(N_FOURIER_GROUPS):
                ab = _dot(x_ref[b, rs, g * gd:(g + 1) * gd], chan_ref[...])
                ab_ref[rs, g * gd:(g + 1) * gd] = ab[:, :gd].astype(BF16)
                ab_ref[s + r * row_chunk: s + (r + 1) * row_chunk, g * gd:(g + 1) * gd] = ab[:, gd:].astype(BF16)
        for r in range(s // row_chunk):
            rs = slice(r * row_chunk, (r + 1) * row_chunk)
            o_ref[b, rs, :] = _dot(pos_ref[rs, :], ab_ref[...]).astype(o_ref.dtype)


def _fourier_mix(u, tables, seqs_per_step):
    b, s, w = u.shape
    chan, pos = tables
    row_chunk = min(s, 512)
    blk = pl.BlockSpec((seqs_per_step, s, w), lambda i: (i, 0, 0))
    return pl.pallas_call(
        functools.partial(_fourier_kernel, row_chunk=row_chunk),
        grid=(b // seqs_per_step,),
        in_specs=[blk, _resident(chan.shape), _resident(pos.shape)],
        out_specs=blk,
        out_shape=jax.ShapeDtypeStruct((b, s, w), BF16),
        scratch_shapes=[pltpu.VMEM((2 * s, w), BF16)],
        compiler_params=_params(1),
        name="fourier_mix",
    )(u, chan, pos)


def _merge_mlp_kernel(*refs, final):
    (x_ref, f_ref, a_ref, gate_ref, ga1_ref, sh2_ref, sc2_ref, ga2_ref, g2_ref,
     wfo_ref, wao_ref, wout_ref, w1_ref, w2_ref) = refs[:14]
    o_ref = refs[-1]
    gate_f = gate_ref[:, :D_MODEL].astype(F32)
    gate_a = gate_ref[:, D_MODEL:].astype(F32)
    m = gate_f * _dot(f_ref[...], wfo_ref[...]) + gate_a * _dot(a_ref[...], wao_ref[...])
    x1 = x_ref[...] + ga1_ref[...] * _dot(m.astype(BF16), wout_ref[...])
    h = (_rmsnorm(x1, g2_ref[...]) * (1.0 + sc2_ref[...]) + sh2_ref[...]).astype(BF16)
    ff = None
    for j in range(D_FF // COL_CHUNK):
        cs = slice(j * COL_CHUNK, (j + 1) * COL_CHUNK)
        t = jnp.maximum(_dot(h, w1_ref[:, cs]), 0.0)
        part = _dot((t * t).astype(BF16), w2_ref[cs, :])
        ff = part if ff is None else ff + part
    x2 = x1 + ga2_ref[...] * ff
    if final:
        x2 = _rmsnorm(x2, refs[14][...])
    o_ref[...] = x2


def _merge_mlp(x, f, attn, gates, mod, layer, row_of_tile, g2, w_fo, w_ao, w_out, w_ff1, w_ff2, final_g, tm):
    rows = x.shape[0]
    tile = lambda w: pl.BlockSpec((tm, w), lambda i: (i, 0))
    in_specs = [tile(D_MODEL), tile(FOURIER_WIDTH), tile(ATTN_WIDTH), tile(2 * D_MODEL)]
    in_specs += [_mod_spec(layer, slot, row_of_tile) for slot in (2, 3, 4, 5)]
    in_specs += [_resident((1, D_MODEL), layer)]
    in_specs += [_resident(w.shape[1:], layer) for w in (w_fo, w_ao, w_out, w_ff1, w_ff2)]
    args = [x, f, attn, gates, mod, mod, mod, mod, g2, w_fo, w_ao, w_out, w_ff1, w_ff2]
    if final_g is not None:
        in_specs.append(_resident((1, D_MODEL)))
        args.append(final_g)
    return pl.pallas_call(
        functools.partial(_merge_mlp_kernel, final=final_g is not None),
        grid=(rows // tm,),
        in_specs=in_specs,
        out_specs=tile(D_MODEL),
        out_shape=jax.ShapeDtypeStruct((rows, D_MODEL), F32),
        compiler_params=_params(1),
        name="merge_mlp",
    )(*args)


def _rope_tables(n_tokens):
    t = jnp.arange(n_tokens)
    row = (t // GRID_W).astype(F32)
    col = (t % GRID_W).astype(F32)
    inv = ROPE_THETA ** (-jnp.arange(ROPE_PAIRS_PER_AXIS, dtype=F32) / ROPE_PAIRS_PER_AXIS)
    ang = jnp.concatenate([row[:, None] * inv[None, :], col[:, None] * inv[None, :]], axis=-1)
    cos = jnp.repeat(jnp.cos(ang), 2, axis=1)
    sin = jnp.repeat(jnp.sin(ang), 2, axis=1)
    even = (jnp.arange(HEAD_DIM) % 2 == 0)[None, :]
    reps = LANES // HEAD_DIM
    return (jnp.tile(cos, (1, reps)),
            jnp.tile(jnp.where(even, -sin, 0.0), (1, reps)),
            jnp.tile(jnp.where(even, 0.0, sin), (1, reps)))


def kernel(x_prompt, x_sample, c, cache_k, cache_v, c_ctx, w_ada, b_ada, norm1_g, norm2_g,
           w_in, sink, w_fo, w_ao, w_out, w_ff1, w_ff2, final_g):
    depth = w_in.shape[0]
    b_ctx, s_ctx, _ = x_prompt.shape
    b_lat, s_lat, _ = x_sample.shape
    past = cache_k.shape[2]
    assert 1 + b_lat <= MOD_ROWS
    tm, tm_in = ROW_TILE, INPROJ_ROW_TILE
    assert (b_ctx * s_ctx) % tm == 0 and s_lat % tm == 0 and (b_ctx * s_ctx) % tm_in == 0 and s_lat % tm_in == 0
    assert tm_in % SUB_TILE == 0 and SUB_TILE % s_ctx == 0

    cvecs = jnp.zeros((MOD_ROWS, D_MODEL), F32).at[0].set(c_ctx).at[1:1 + b_lat].set(c)
    mod = _modulation(cvecs, w_ada, b_ada).reshape(depth, MOD_ROWS, 6, 1, D_MODEL)
    ctx_row = lambda tile: lambda i: 0
    lat_row = lambda tile: lambda i: 1 + i // (s_lat // tile)

    w_in_b, w_fo_b, w_ao_b, w_out_b, w_ff1_b, w_ff2_b = (
        w.astype(BF16) for w in (w_in, w_fo, w_ao, w_out, w_ff1, w_ff2))
    w_v_t = jnp.swapaxes(w_in_b[:, :, OFF_V:OFF_G], 1, 2)
    g1 = norm1_g.reshape(depth, 1, D_MODEL)
    g2 = norm2_g.reshape(depth, 1, D_MODEL)
    gf = final_g.reshape(1, D_MODEL)
    rope_tabs = _rope_tables(s_lat)
    dft_ctx = _dft_tables(s_ctx)
    dft_lat = _dft_tables(s_lat)
    cache_k = cache_k.reshape(b_lat, depth, past, KV_WIDTH)
    cache_v = cache_v.reshape(b_lat, depth, past, KV_WIDTH)

    def layer_tail(x, uf, attn, gates, l, row_of_tile, fourier_tabs, batch, seqs_per_step):
        seq = x.shape[0] // batch
        f = _fourier_mix(uf.reshape(batch, seq, FOURIER_WIDTH), fourier_tabs, seqs_per_step)
        return _merge_mlp(x, f.reshape(-1, FOURIER_WIDTH), attn.reshape(-1, ATTN_WIDTH), gates, mod, l, row_of_tile,
                          g2, w_fo_b, w_ao_b, w_out_b, w_ff1_b, w_ff2_b, gf if l == depth - 1 else None, tm)

    xc = x_prompt.reshape(b_ctx * s_ctx, D_MODEL)
    kv_cache = (jnp.zeros((b_ctx, depth, s_ctx, KV_WIDTH), F32),) * 2
    for l in range(depth):
        uf, q, *kv_cache, gates = _in_projection(xc, mod, l, ctx_row(tm_in), g1, w_in_b, tm_in, kv_cache=kv_cache)
        attn = _context_attention(q.reshape(b_ctx, s_ctx, ATTN_WIDTH), *kv_cache, sink, l)
        xc = layer_tail(xc, uf, attn, gates, l, ctx_row(tm), dft_ctx, b_ctx, 4 if b_ctx % 4 == 0 else 1)
    y_prompt = xc.reshape(b_ctx, s_ctx, D_MODEL)
    new_k, new_v = (buf.reshape(b_ctx, depth, s_ctx, N_KV_HEADS, HEAD_DIM) for buf in kv_cache)

    xs = x_sample.reshape(b_lat * s_lat, D_MODEL)
    for l in range(depth):
        uf, q, k, v_t, gates = _in_projection(xs, mod, l, lat_row(tm_in), g1, w_in_b, tm_in, w_v_t=w_v_t,
                                              rope_tabs=rope_tabs)
        attn = _latent_attention(q.reshape(b_lat, s_lat, ATTN_WIDTH), k.reshape(b_lat, s_lat, KV_WIDTH),
                                 v_t, cache_k, cache_v, sink, l)
        xs = layer_tail(xs, uf, attn, gates, l, lat_row(tm), dft_lat, b_lat, 1)
    y_sample = xs.reshape(b_lat, s_lat, D_MODEL)

    return y_prompt, y_sample, new_k, new_v
```

```python
import functools

import numpy as np
import jax
import jax.numpy as jnp
from jax import lax
from jax.experimental import pallas as pl
from jax.experimental.pallas import tpu as pltpu

D_MODEL = 1024
N_HEADS = 16
N_KV_HEADS = 4
HEAD_DIM = 64
Q_PER_KV = N_HEADS // N_KV_HEADS
ATTN_WIDTH = N_HEADS * HEAD_DIM
KV_WIDTH = N_KV_HEADS * HEAD_DIM
N_FOURIER_GROUPS = 4
FOURIER_GROUP_DIM = 128
FOURIER_WIDTH = N_FOURIER_GROUPS * FOURIER_GROUP_DIM
IN_WIDTH = FOURIER_WIDTH + ATTN_WIDTH + 2 * KV_WIDTH + 2 * D_MODEL
D_FF = 4 * D_MODEL
GRID_W = 64
WINDOW = 128
ROPE_THETA = 10000.0
ROPE_PAIRS_PER_AXIS = HEAD_DIM // 4
EPS = 1e-6
NEG_INF = -1e30
LOG2E = 1.4426950408889634
QUERY_SCALE = HEAD_DIM ** -0.5 * LOG2E

OFF_Q = FOURIER_WIDTH
OFF_K = OFF_Q + ATTN_WIDTH
OFF_V = OFF_K + KV_WIDTH
OFF_G = OFF_V + KV_WIDTH

LANES = 128
MOD_ROWS = 16
VMEM_LIMIT_BYTES = 56 * 1024 * 1024
ROW_TILE = 512
INPROJ_ROW_TILE = 1024
SUB_TILE = 512
COL_CHUNK = 512
Q_BLOCK = 128
KEY_SPAN = 3 * Q_BLOCK
MAX_ROWS = 128
EXP_ROWS = 64
LATENT_QBLOCKS_PER_STEP = 8
CTX_SEQS_PER_STEP = 8
SCRATCH_SETS = 2
SYMMETRIC_MIN_SEQ = 1024
TABLE_PAD = 16
ONES_ROWS = 16
AUG_ROWS = HEAD_DIM + ONES_ROWS

BF16 = jnp.bfloat16
F32 = jnp.float32


def _params(n_axes):
    return pltpu.CompilerParams(dimension_semantics=("arbitrary",) * n_axes,
                                vmem_limit_bytes=VMEM_LIMIT_BYTES)


def _resident(shape, layer=None):
    zeros = (0,) * len(shape)
    if layer is None:
        return pl.BlockSpec(shape, lambda *_: zeros, pipeline_mode=pl.Buffered(1))
    return pl.BlockSpec((None,) + tuple(shape), lambda *_: (layer,) + zeros, pipeline_mode=pl.Buffered(1))


def _dot(a, b):
    return jnp.dot(a, b, preferred_element_type=F32)


def _sigmoid(z):
    return 1.0 / (1.0 + jnp.exp(-z))


def _rmsnorm(x, g):
    return x * lax.rsqrt(jnp.mean(x * x, axis=-1, keepdims=True) + EPS) * g


def _mod_kernel(c_ref, w_ref, b_ref, o_ref):
    c = c_ref[...]
    a = c * _sigmoid(c)
    a_hi = a.astype(BF16)
    a_lo = (a - a_hi.astype(F32)).astype(BF16)
    w = w_ref[...]
    w_hi = w.astype(BF16)
    w_lo = (w - w_hi.astype(F32)).astype(BF16)
    o_ref[...] = _dot(a_hi, w_hi) + _dot(a_lo, w_hi) + _dot(a_hi, w_lo) + b_ref[...]


def _modulation(cvecs, w_ada, b_ada):
    depth, _, width = w_ada.shape
    tn = width // 4
    return pl.pallas_call(
        _mod_kernel,
        grid=(depth, width // tn),
        in_specs=[
            pl.BlockSpec((MOD_ROWS, D_MODEL), lambda l, j: (0, 0)),
            pl.BlockSpec((None, D_MODEL, tn), lambda l, j: (l, 0, j)),
            pl.BlockSpec((None, 1, tn), lambda l, j: (l, 0, j)),
        ],
        out_specs=pl.BlockSpec((None, MOD_ROWS, tn), lambda l, j: (l, 0, j)),
        out_shape=jax.ShapeDtypeStruct((depth, MOD_ROWS, width), F32),
        compiler_params=_params(2),
        name="modulation",
    )(cvecs, w_ada, b_ada.reshape(depth, 1, width))


def _mod_spec(layer, slot, row_of_tile):
    return pl.BlockSpec((None, None, None, 1, D_MODEL),
                        lambda i: (layer, row_of_tile(i), slot, 0, 0))


def _rope(z, cos, sin_even, sin_odd):
    outs = []
    for j in range(z.shape[1] // LANES):
        zj = z[:, j * LANES:(j + 1) * LANES]
        outs.append(zj * cos + pltpu.roll(zj, LANES - 1, 1) * sin_even + pltpu.roll(zj, 1, 1) * sin_odd)
    return outs[0] if len(outs) == 1 else jnp.concatenate(outs, axis=1)


def _augment_values_t(v_t):
    ones = jnp.ones((ONES_ROWS, v_t.shape[1]), v_t.dtype)
    parts = []
    for kv in range(N_KV_HEADS):
        parts += [v_t[kv * HEAD_DIM:(kv + 1) * HEAD_DIM], ones]
    return jnp.concatenate(parts, axis=0).astype(BF16)


def _inproj_kernel(*refs, rope):
    if rope:
        (x_ref, sh_ref, sc_ref, g_ref, w_ref, wvt_ref, cos_ref, se_ref, so_ref,
         uf_ref, q_ref, k_ref, v_ref, gate_ref) = refs
        rot = lambda z, rows: _rope(z, cos_ref[rows, :], se_ref[rows, :], so_ref[rows, :])
    else:
        x_ref, sh_ref, sc_ref, g_ref, w_ref, _, _, uf_ref, q_ref, k_ref, v_ref, gate_ref = refs
        rot = lambda z, rows: z
    for r0 in range(0, x_ref.shape[0], SUB_TILE):
        rows = slice(r0, r0 + SUB_TILE)
        h = _rmsnorm(x_ref[rows, :], g_ref[...]) * (1.0 + sc_ref[...]) + sh_ref[...]
        h = h.astype(BF16)

        def seg(start, h=h):
            return _dot(h, w_ref[:, start:start + COL_CHUNK])

        uf_ref[rows, :] = seg(0).astype(uf_ref.dtype)
        for j in range(ATTN_WIDTH // COL_CHUNK):
            z = rot(seg(OFF_Q + j * COL_CHUNK), rows) * QUERY_SCALE
            q_ref[rows, j * COL_CHUNK:(j + 1) * COL_CHUNK] = z.astype(q_ref.dtype)
        if rope:
            k_ref[rows, :] = rot(_dot(h, w_ref[:, OFF_K:OFF_V]), rows).astype(k_ref.dtype)
            v_t = lax.dot_general(wvt_ref[...], h, (((1,), (1,)), ((), ())), preferred_element_type=F32)
            v_ref[:, rows] = _augment_values_t(v_t)
        else:
            z = seg(OFF_K)
            seqs = slice(r0 // k_ref.shape[1], (r0 + SUB_TILE) // k_ref.shape[1])
            k_ref[seqs] = z[:, :KV_WIDTH].reshape(k_ref[seqs].shape)
            v_ref[seqs] = z[:, KV_WIDTH:].reshape(v_ref[seqs].shape)
        for j in range(2 * D_MODEL // COL_CHUNK):
            z = seg(OFF_G + j * COL_CHUNK)
            gate_ref[rows, j * COL_CHUNK:(j + 1) * COL_CHUNK] = _sigmoid(z).astype(gate_ref.dtype)


def _in_projection(x, mod, layer, row_of_tile, g1, w_in, tm, *, w_v_t=None, rope_tabs=None, kv_cache=None):
    rows = x.shape[0]
    tile = lambda w: pl.BlockSpec((tm, w), lambda i: (i, 0))
    in_specs = [tile(D_MODEL), _mod_spec(layer, 0, row_of_tile), _mod_spec(layer, 1, row_of_tile),
                _resident((1, D_MODEL), layer), _resident((D_MODEL, IN_WIDTH), layer)]
    args = [x, mod, mod, g1, w_in]
    out_specs = [tile(FOURIER_WIDTH), tile(ATTN_WIDTH), None, None, tile(2 * D_MODEL)]
    out_shape = [jax.ShapeDtypeStruct((rows, FOURIER_WIDTH), BF16), jax.ShapeDtypeStruct((rows, ATTN_WIDTH), BF16),
                 None, None, jax.ShapeDtypeStruct((rows, 2 * D_MODEL), BF16)]
    latent = rope_tabs is not None
    aliases = {}
    if latent:
        tiles_per_seq = rope_tabs[0].shape[0] // tm
        in_specs += [_resident((KV_WIDTH, D_MODEL), layer)]
        in_specs += [pl.BlockSpec((tm, LANES), lambda i: (i % tiles_per_seq, 0))] * 3
        args += [w_v_t] + list(rope_tabs)
        out_specs[2] = tile(KV_WIDTH)
        out_shape[2] = jax.ShapeDtypeStruct((rows, KV_WIDTH), BF16)
        out_specs[3] = pl.BlockSpec((N_KV_HEADS * AUG_ROWS, tm), lambda i: (0, i))
        out_shape[3] = jax.ShapeDtypeStruct((N_KV_HEADS * AUG_ROWS, rows), BF16)
    else:
        s_ctx = kv_cache[0].shape[2]
        assert tm % s_ctx == 0
        in_specs += [pl.BlockSpec(memory_space=pl.ANY)] * 2
        args += list(kv_cache)
        aliases = {5: 2, 6: 3}
        for slot in (2, 3):
            out_specs[slot] = pl.BlockSpec((tm // s_ctx, None, s_ctx, KV_WIDTH), lambda i: (i, layer, 0, 0))
            out_shape[slot] = jax.ShapeDtypeStruct(kv_cache[0].shape, F32)
    return pl.pallas_call(
        functools.partial(_inproj_kernel, rope=latent),
        grid=(rows // tm,),
        in_specs=in_specs,
        out_specs=out_specs,
        out_shape=out_shape,
        input_output_aliases=aliases,
        compiler_params=_params(1),
        name="in_projection",
    )(*args)


def _softmax_head(s_ref, p_ref, cols, bias_t_of, n_biased, sink):
    n_keys = s_ref.shape[0]
    blocks = []
    for r in range(0, n_keys, MAX_ROWS):
        rows = slice(r, r + MAX_ROWS)
        blk = s_ref[rows, cols]
        blocks.append(blk + bias_t_of(rows) if r < n_biased else blk)
    m = jnp.maximum(jnp.max(functools.reduce(jnp.maximum, blocks), axis=0, keepdims=True), sink)
    for r in range(0, n_keys, EXP_ROWS):
        rows = slice(r, r + EXP_ROWS)
        z = s_ref[rows, cols] - m
        if r < n_biased:
            z = z + bias_t_of(rows)
        p_ref[rows, cols] = jnp.exp2(z).astype(BF16)
    return jnp.exp2(sink - m)


def _attend(q_of, key_parts, bias_t_of, n_biased, sink_of, s_ref, p_ref, o_ref):
    t = o_ref.shape[0]

    def score(kv):
        qs = jnp.concatenate([q_of(kv * Q_PER_KV + g) for g in range(Q_PER_KV)], axis=0)
        row = 0
        for k_of, _ in key_parts:
            k = k_of(kv)
            s_ref[kv, row:row + k.shape[0], :] = lax.dot_general(k, qs, (((1,), (1,)), ((), ())),
                                                                 preferred_element_type=F32)
            row += k.shape[0]

    score(0)
    for kv in range(N_KV_HEADS):
        if kv + 1 < N_KV_HEADS:
            score(kv + 1)
        sink_terms = []
        for g in range(Q_PER_KV):
            sink = sink_of(kv * Q_PER_KV + g) * LOG2E
            sink_terms.append(_softmax_head(s_ref.at[kv], p_ref.at[kv], slice(g * t, (g + 1) * t),
                                            bias_t_of, n_biased, sink))
        o_aug, row = None, 0
        for _, v_t_of in key_parts:
            v_t = v_t_of(kv)
            part = _dot(v_t, p_ref[kv, row:row + v_t.shape[1], :])
            o_aug = part if o_aug is None else o_aug + part
            row += v_t.shape[1]
        denom = o_aug[HEAD_DIM:HEAD_DIM + 1, :] + jnp.concatenate(sink_terms, axis=1)
        o_t = o_aug[:HEAD_DIM, :] / denom
        for pair in range(Q_PER_KV // 2):
            two = jnp.concatenate([o_t[:, (2 * pair) * t:(2 * pair + 1) * t],
                                   o_t[:, (2 * pair + 1) * t:(2 * pair + 2) * t]], axis=0)
            col = (kv * (Q_PER_KV // 2) + pair) * LANES
            o_ref[:, col:col + LANES] = two.T.astype(o_ref.dtype)


def _head_cols(head):
    return slice(head * HEAD_DIM, (head + 1) * HEAD_DIM)


def _attn_ctx_kernel(sink_ref, q_ref, k_ref, v_ref, o_ref, s_ref, p_ref, *, layer):
    for b in range(q_ref.shape[0]):
        k = k_ref[b].astype(BF16)
        v_t = _augment_values_t(v_ref[b].T)
        _attend(lambda h: q_ref[b, :, _head_cols(h)],
                [(lambda kv: k[:, _head_cols(kv)], lambda kv: v_t[kv * AUG_ROWS:(kv + 1) * AUG_ROWS])],
                None, 0, lambda h: sink_ref[layer, h], s_ref.at[b % SCRATCH_SETS], p_ref.at[b % SCRATCH_SETS], o_ref.at[b])


def _context_attention(q, k, v, sink, layer):
    b, s, _ = q.shape
    nb = CTX_SEQS_PER_STEP
    assert s % MAX_ROWS == 0 and b % nb == 0
    blk = pl.BlockSpec((nb, s, ATTN_WIDTH), lambda i: (i, 0, 0))
    cache = pl.BlockSpec((nb, None, s, KV_WIDTH), lambda i: (i, layer, 0, 0))
    return pl.pallas_call(
        functools.partial(_attn_ctx_kernel, layer=layer),
        grid=(b // nb,),
        in_specs=[pl.BlockSpec(memory_space=pltpu.SMEM), blk, cache, cache],
        out_specs=blk,
        out_shape=jax.ShapeDtypeStruct((b, s, ATTN_WIDTH), BF16),
        scratch_shapes=[pltpu.VMEM((SCRATCH_SETS, N_KV_HEADS, s, Q_PER_KV * s), F32),
                        pltpu.VMEM((SCRATCH_SETS, N_KV_HEADS, s, Q_PER_KV * s), BF16)],
        compiler_params=_params(1),
        name="context_attention",
    )(sink, q, k, v)


def _window_bias_t():
    c = np.arange(KEY_SPAN)[None, :, None]
    r = np.arange(Q_BLOCK)[None, None, :]
    p = np.arange(KEY_SPAN // Q_BLOCK)[:, None, None]
    return jnp.asarray(np.where(np.abs(c - (r + p * Q_BLOCK)) <= WINDOW, 0.0, NEG_INF).astype(np.float32))


def _attn_lat_kernel(sink_ref, q_ref, k_ref, vt_ref, kc_ref, vc_ref, bias_ref, o_ref, kcb_ref, vcb_ref, s_ref, p_ref,
                     *, layer):
    t = Q_BLOCK
    s_len = k_ref.shape[0]

    @pl.when(pl.program_id(1) == 0)
    def _():
        kcb_ref[...] = kc_ref[...].astype(BF16)
        vcb_ref[...] = _augment_values_t(vc_ref[...].T)

    context = (lambda kv: kcb_ref[:, _head_cols(kv)], lambda kv: vcb_ref[kv * AUG_ROWS:(kv + 1) * AUG_ROWS, :])
    for j in range(q_ref.shape[0] // t):
        i = pl.program_id(1) * (q_ref.shape[0] // t) + j
        start = pl.multiple_of(jnp.clip((i - 1) * t, 0, s_len - KEY_SPAN), t)
        pattern = lax.div(i * t - start, t)
        window = (lambda kv, start=start: k_ref[pl.ds(start, KEY_SPAN), _head_cols(kv)],
                  lambda kv, start=start: vt_ref[kv * AUG_ROWS:(kv + 1) * AUG_ROWS, pl.ds(start, KEY_SPAN)])
        qrows = slice(j * t, (j + 1) * t)
        _attend(lambda h, qrows=qrows: q_ref[qrows, _head_cols(h)], [window, context],
                lambda rows, pattern=pattern: bias_ref[pattern, rows, :], KEY_SPAN,
                lambda h: sink_ref[layer, h], s_ref.at[j % SCRATCH_SETS], p_ref.at[j % SCRATCH_SETS], o_ref.at[qrows])


def _latent_attention(q, k, v_t, cache_k, cache_v, sink, layer):
    b, s, _ = q.shape
    past = cache_k.shape[2]
    nq = LATENT_QBLOCKS_PER_STEP
    assert s % (nq * Q_BLOCK) == 0 and s >= KEY_SPAN and WINDOW <= Q_BLOCK and Q_BLOCK == LANES
    assert past % MAX_ROWS == 0
    n_keys = KEY_SPAN + past
    qblk = pl.BlockSpec((None, nq * Q_BLOCK, ATTN_WIDTH), lambda bi, i: (bi, i, 0))
    cache = pl.BlockSpec((None, None, past, KV_WIDTH), lambda bi, i: (bi, layer, 0, 0))
    bias = _window_bias_t()
    return pl.pallas_call(
        functools.partial(_attn_lat_kernel, layer=layer),
        grid=(b, s // (nq * Q_BLOCK)),
        in_specs=[pl.BlockSpec(memory_space=pltpu.SMEM), qblk,
                  pl.BlockSpec((None, s, KV_WIDTH), lambda bi, i: (bi, 0, 0)),
                  pl.BlockSpec((N_KV_HEADS * AUG_ROWS, s), lambda bi, i: (0, bi)),
                  cache, cache, _resident(bias.shape)],
        out_specs=qblk,
        out_shape=jax.ShapeDtypeStruct((b, s, ATTN_WIDTH), BF16),
        scratch_shapes=[pltpu.VMEM((past, KV_WIDTH), BF16), pltpu.VMEM((N_KV_HEADS * AUG_ROWS, past), BF16),
                        pltpu.VMEM((SCRATCH_SETS, N_KV_HEADS, n_keys, Q_PER_KV * Q_BLOCK), F32),
                        pltpu.VMEM((SCRATCH_SETS, N_KV_HEADS, n_keys, Q_PER_KV * Q_BLOCK), BF16)],
        compiler_params=_params(2),
        name="latent_attention",
    )(sink, q, k, v_t, cache_k, cache_v, bias)


def _dft_tables(s):
    def cos_sin(n, rows):
        k = np.arange(rows, dtype=np.int64)[:, None]
        j = np.arange(n, dtype=np.int64)[None, :]
        ang = 2.0 * np.pi * ((k * j) % n) / n
        scale = n ** -0.5
        return (np.cos(ang) * scale).astype(np.float32), (np.sin(ang) * scale).astype(np.float32)
    as_bf16 = lambda t: jnp.asarray(t).astype(BF16)
    cc, sc = cos_sin(FOURIER_GROUP_DIM, FOURIER_GROUP_DIM)
    chan = as_bf16(np.concatenate([cc, sc], axis=1))
    if s < SYMMETRIC_MIN_SEQ:
        cs, ss = cos_sin(s, s)
        return chan, as_bf16(cs), as_bf16(ss)
    half = s // 2
    cs, ss = cos_sin(s, half + TABLE_PAD)
    rev = np.zeros((half, half), np.float32)
    rev[np.arange(1, half), half - np.arange(1, half)] = 1.0
    return chan, as_bf16(cs), as_bf16(ss[:half]), as_bf16(rev)


def _fourier_kernel(x_ref, chan_ref, cos_ref, sin_ref, *rest, row_chunk):
    *rev_ref, o_ref, ab_ref, eo_ref = rest
    nb, s, _ = x_ref.shape
    n_direct = sin_ref.shape[0]
    gd = FOURIER_GROUP_DIM
    for b in range(nb):
        for r in range(0, s, row_chunk):
            for g in range(N_FOURIER_GROUPS):
                ab = _dot(x_ref[b, r:r + row_chunk, g * gd:(g + 1) * gd], chan_ref[...])
                ab_ref[r:r + row_chunk, g * gd:(g + 1) * gd] = ab[:, :gd].astype(BF16)
                ab_ref[s + r:s + r + row_chunk, g * gd:(g + 1) * gd] = ab[:, gd:].astype(BF16)
        for r in range(0, n_direct, row_chunk):
            rows = slice(r, r + row_chunk)
            e = _dot(cos_ref[rows, :], ab_ref[:s, :])
            o = _dot(sin_ref[rows, :], ab_ref[s:, :])
            o_ref[b, rows, :] = (e - o).astype(o_ref.dtype)
            if rev_ref:
                eo_ref[rows, :] = (e + o).astype(BF16)
        if rev_ref:
            e_mid = _dot(cos_ref[n_direct:n_direct + TABLE_PAD, :], ab_ref[:s, :])
            first = lax.broadcasted_iota(jnp.int32, e_mid.shape, 0) == 0
            for r in range(0, n_direct, row_chunk):
                up = _dot(rev_ref[0][r:r + row_chunk, :], eo_ref[...])
                if r == 0:
                    up = jnp.concatenate([up[:TABLE_PAD] + jnp.where(first, e_mid, 0.0), up[TABLE_PAD:]], axis=0)
                o_ref[b, n_direct + r:n_direct + r + row_chunk, :] = up.astype(o_ref.dtype)


def _fourier_mix(u, tables, seqs_per_step):
    b, s, w = u.shape
    n_direct = tables[2].shape[0]
    row_chunk = min(n_direct, 512)
    blk = pl.BlockSpec((seqs_per_step, s, w), lambda i: (i, 0, 0))
    return pl.pallas_call(
        functools.partial(_fourier_kernel, row_chunk=row_chunk),
        grid=(b // seqs_per_step,),
        in_specs=[blk] + [_resident(t.shape) for t in tables],
        out_specs=blk,
        out_shape=jax.ShapeDtypeStruct((b, s, w), BF16),
        scratch_shapes=[pltpu.VMEM((2 * s, w), BF16), pltpu.VMEM((n_direct, w), BF16)],
        compiler_params=_params(1),
        name="fourier_mix",
    )(u, *tables)


def _merge_mlp_kernel(*refs, final):
    (x_ref, f_ref, a_ref, gate_ref, ga1_ref, sh2_ref, sc2_ref, ga2_ref, g2_ref,
     wfo_ref, wao_ref, wout_ref, w1_ref, w2_ref) = refs[:14]
    o_ref = refs[-1]
    gate_f = gate_ref[:, :D_MODEL].astype(F32)
    gate_a = gate_ref[:, D_MODEL:].astype(F32)
    m = gate_f * _dot(f_ref[...], wfo_ref[...]) + gate_a * _dot(a_ref[...], wao_ref[...])
    x1 = x_ref[...] + ga1_ref[...] * _dot(m.astype(BF16), wout_ref[...])
    h = (_rmsnorm(x1, g2_ref[...]) * (1.0 + sc2_ref[...]) + sh2_ref[...]).astype(BF16)
    ff = None
    for j in range(D_FF // COL_CHUNK):
        cs = slice(j * COL_CHUNK, (j + 1) * COL_CHUNK)
        t = jnp.maximum(_dot(h, w1_ref[:, cs]), 0.0)
        part = _dot((t * t).astype(BF16), w2_ref[cs, :])
        ff = part if ff is None else ff + part
    x2 = x1 + ga2_ref[...] * ff
    if final:
        x2 = _rmsnorm(x2, refs[14][...])
    o_ref[...] = x2


def _merge_mlp(x, f, attn, gates, mod, layer, row_of_tile, g2, w_fo, w_ao, w_out, w_ff1, w_ff2, final_g, tm):
    rows = x.shape[0]
    tile = lambda w: pl.BlockSpec((tm, w), lambda i: (i, 0))
    in_specs = [tile(D_MODEL), tile(FOURIER_WIDTH), tile(ATTN_WIDTH), tile(2 * D_MODEL)]
    in_specs += [_mod_spec(layer, slot, row_of_tile) for slot in (2, 3, 4, 5)]
    in_specs += [_resident((1, D_MODEL), layer)]
    in_specs += [_resident(w.shape[1:], layer) for w in (w_fo, w_ao, w_out, w_ff1, w_ff2)]
    args = [x, f, attn, gates, mod, mod, mod, mod, g2, w_fo, w_ao, w_out, w_ff1, w_ff2]
    if final_g is not None:
        in_specs.append(_resident((1, D_MODEL)))
        args.append(final_g)
    return pl.pallas_call(
        functools.partial(_merge_mlp_kernel, final=final_g is not None),
        grid=(rows // tm,),
        in_specs=in_specs,
        out_specs=tile(D_MODEL),
        out_shape=jax.ShapeDtypeStruct((rows, D_MODEL), F32),
        compiler_params=_params(1),
        name="merge_mlp",
    )(*args)


def _rope_tables(n_tokens):
    t = jnp.arange(n_tokens)
    row = (t // GRID_W).astype(F32)
    col = (t % GRID_W).astype(F32)
    inv = ROPE_THETA ** (-jnp.arange(ROPE_PAIRS_PER_AXIS, dtype=F32) / ROPE_PAIRS_PER_AXIS)
    ang = jnp.concatenate([row[:, None] * inv[None, :], col[:, None] * inv[None, :]], axis=-1)
    cos = jnp.repeat(jnp.cos(ang), 2, axis=1)
    sin = jnp.repeat(jnp.sin(ang), 2, axis=1)
    even = (jnp.arange(HEAD_DIM) % 2 == 0)[None, :]
    reps = LANES // HEAD_DIM
    return (jnp.tile(cos, (1, reps)),
            jnp.tile(jnp.where(even, -sin, 0.0), (1, reps)),
            jnp.tile(jnp.where(even, 0.0, sin), (1, reps)))


def kernel(x_prompt, x_sample, c, cache_k, cache_v, c_ctx, w_ada, b_ada, norm1_g, norm2_g,
           w_in, sink, w_fo, w_ao, w_out, w_ff1, w_ff2, final_g):
    depth = w_in.shape[0]
    b_ctx, s_ctx, _ = x_prompt.shape
    b_lat, s_lat, _ = x_sample.shape
    past = cache_k.shape[2]
    assert 1 + b_lat <= MOD_ROWS
    tm, tm_in = ROW_TILE, INPROJ_ROW_TILE
    assert (b_ctx * s_ctx) % tm == 0 and s_lat % tm == 0 and (b_ctx * s_ctx) % tm_in == 0 and s_lat % tm_in == 0
    assert tm_in % SUB_TILE == 0 and SUB_TILE % s_ctx == 0

    cvecs = jnp.zeros((MOD_ROWS, D_MODEL), F32).at[0].set(c_ctx).at[1:1 + b_lat].set(c)
    mod = _modulation(cvecs, w_ada, b_ada).reshape(depth, MOD_ROWS, 6, 1, D_MODEL)
    ctx_row = lambda tile: lambda i: 0
    lat_row = lambda tile: lambda i: 1 + i // (s_lat // tile)

    w_in_b, w_fo_b, w_ao_b, w_out_b, w_ff1_b, w_ff2_b = (
        w.astype(BF16) for w in (w_in, w_fo, w_ao, w_out, w_ff1, w_ff2))
    w_v_t = jnp.swapaxes(w_in_b[:, :, OFF_V:OFF_G], 1, 2)
    g1 = norm1_g.reshape(depth, 1, D_MODEL)
    g2 = norm2_g.reshape(depth, 1, D_MODEL)
    gf = final_g.reshape(1, D_MODEL)
    rope_tabs = _rope_tables(s_lat)
    dft_ctx = _dft_tables(s_ctx)
    dft_lat = _dft_tables(s_lat)
    cache_k = cache_k.reshape(b_lat, depth, past, KV_WIDTH)
    cache_v = cache_v.reshape(b_lat, depth, past, KV_WIDTH)

    def layer_tail(x, uf, attn, gates, l, row_of_tile, fourier_tabs, batch, seqs_per_step):
        seq = x.shape[0] // batch
        f = _fourier_mix(uf.reshape(batch, seq, FOURIER_WIDTH), fourier_tabs, seqs_per_step)
        return _merge_mlp(x, f.reshape(-1, FOURIER_WIDTH), attn.reshape(-1, ATTN_WIDTH), gates, mod, l, row_of_tile,
                          g2, w_fo_b, w_ao_b, w_out_b, w_ff1_b, w_ff2_b, gf if l == depth - 1 else None, tm)

    xc = x_prompt.reshape(b_ctx * s_ctx, D_MODEL)
    kv_cache = (jnp.zeros((b_ctx, depth, s_ctx, KV_WIDTH), F32),) * 2
    for l in range(depth):
        uf, q, *kv_cache, gates = _in_projection(xc, mod, l, ctx_row(tm_in), g1, w_in_b, tm_in, kv_cache=kv_cache)
        attn = _context_attention(q.reshape(b_ctx, s_ctx, ATTN_WIDTH), *kv_cache, sink, l)
        xc = layer_tail(xc, uf, attn, gates, l, ctx_row(tm), dft_ctx, b_ctx, 4 if b_ctx % 4 == 0 else 1)
    y_prompt = xc.reshape(b_ctx, s_ctx, D_MODEL)
    new_k, new_v = (buf.reshape(b_ctx, depth, s_ctx, N_KV_HEADS, HEAD_DIM) for buf in kv_cache)

    xs = x_sample.reshape(b_lat * s_lat, D_MODEL)
    for l in range(depth):
        uf, q, k, v_t, gates = _in_projection(xs, mod, l, lat_row(tm_in), g1, w_in_b, tm_in, w_v_t=w_v_t,
                                              rope_tabs=rope_tabs)
        attn = _latent_attention(q.reshape(b_lat, s_lat, ATTN_WIDTH), k.reshape(b_lat, s_lat, KV_WIDTH),
                                 v_t, cache_k, cache_v, sink, l)
        xs = layer_tail(xs, uf, attn, gates, l, lat_row(tm), dft_lat, b_lat, 1)
    y_sample = xs.reshape(b_lat, s_lat, D_MODEL)

    return y_prompt, y_sample, new_k, new_v
```

```python
import functools

import numpy as np
import jax
import jax.numpy as jnp
from jax import lax
from jax.experimental import pallas as pl
from jax.experimental.pallas import tpu as pltpu

D_MODEL = 1024
N_HEADS = 16
N_KV_HEADS = 4
HEAD_DIM = 64
Q_PER_KV = N_HEADS // N_KV_HEADS
ATTN_WIDTH = N_HEADS * HEAD_DIM
KV_WIDTH = N_KV_HEADS * HEAD_DIM
N_FOURIER_GROUPS = 4
FOURIER_GROUP_DIM = 128
FOURIER_WIDTH = N_FOURIER_GROUPS * FOURIER_GROUP_DIM
IN_WIDTH = FOURIER_WIDTH + ATTN_WIDTH + 2 * KV_WIDTH + 2 * D_MODEL
D_FF = 4 * D_MODEL
GRID_W = 64
WINDOW = 128
ROPE_THETA = 10000.0
ROPE_PAIRS_PER_AXIS = HEAD_DIM // 4
EPS = 1e-6
NEG_INF = -1e30
LOG2E = 1.4426950408889634
QUERY_SCALE = HEAD_DIM ** -0.5 * LOG2E

OFF_Q = FOURIER_WIDTH
OFF_K = OFF_Q + ATTN_WIDTH
OFF_V = OFF_K + KV_WIDTH
OFF_G = OFF_V + KV_WIDTH

LANES = 128
MOD_ROWS = 16
VMEM_LIMIT_BYTES = 56 * 1024 * 1024
ROW_TILE = 512
INPROJ_ROW_TILE = 1024
SUB_TILE = 512
COL_CHUNK = 512
Q_BLOCK = 128
KEY_SPAN = 3 * Q_BLOCK
MAX_ROWS = 128
EXP_ROWS = 64
LATENT_QBLOCKS_PER_STEP = 8
CTX_SEQS_PER_STEP = 8
SCRATCH_SETS = 2
SYMMETRIC_MIN_SEQ = 1024
TABLE_PAD = 16
ONES_ROWS = 16
AUG_ROWS = HEAD_DIM + ONES_ROWS

BF16 = jnp.bfloat16
F32 = jnp.float32


def _params(n_axes):
    return pltpu.CompilerParams(dimension_semantics=("arbitrary",) * n_axes,
                                vmem_limit_bytes=VMEM_LIMIT_BYTES)


def _resident(shape, layer=None):
    zeros = (0,) * len(shape)
    if layer is None:
        return pl.BlockSpec(shape, lambda *_: zeros, pipeline_mode=pl.Buffered(1))
    return pl.BlockSpec((None,) + tuple(shape), lambda *_: (layer,) + zeros, pipeline_mode=pl.Buffered(1))


def _dot(a, b):
    return jnp.dot(a, b, preferred_element_type=F32)


def _sigmoid(z):
    return 1.0 / (1.0 + jnp.exp(-z))


def _rmsnorm(x, g):
    return x * lax.rsqrt(jnp.mean(x * x, axis=-1, keepdims=True) + EPS) * g


def _mod_kernel(c_ref, w_ref, b_ref, o_ref):
    c = c_ref[...]
    a = c * _sigmoid(c)
    a_hi = a.astype(BF16)
    a_lo = (a - a_hi.astype(F32)).astype(BF16)
    w = w_ref[...]
    w_hi = w.astype(BF16)
    w_lo = (w - w_hi.astype(F32)).astype(BF16)
    o_ref[...] = _dot(a_hi, w_hi) + _dot(a_lo, w_hi) + _dot(a_hi, w_lo) + b_ref[...]


def _modulation(cvecs, w_ada, b_ada):
    depth, _, width = w_ada.shape
    tn = width // 4
    return pl.pallas_call(
        _mod_kernel,
        grid=(depth, width // tn),
        in_specs=[
            pl.BlockSpec((MOD_ROWS, D_MODEL), lambda l, j: (0, 0)),
            pl.BlockSpec((None, D_MODEL, tn), lambda l, j: (l, 0, j)),
            pl.BlockSpec((None, 1, tn), lambda l, j: (l, 0, j)),
        ],
        out_specs=pl.BlockSpec((None, MOD_ROWS, tn), lambda l, j: (l, 0, j)),
        out_shape=jax.ShapeDtypeStruct((depth, MOD_ROWS, width), F32),
        compiler_params=_params(2),
        name="modulation",
    )(cvecs, w_ada, b_ada.reshape(depth, 1, width))


def _mod_spec(layer, slot, row_of_tile):
    return pl.BlockSpec((None, None, None, 1, D_MODEL),
                        lambda i: (layer, row_of_tile(i), slot, 0, 0))


def _rope(z, cos, sin_even, sin_odd):
    outs = []
    for j in range(z.shape[1] // LANES):
        zj = z[:, j * LANES:(j + 1) * LANES]
        outs.append(zj * cos + pltpu.roll(zj, LANES - 1, 1) * sin_even + pltpu.roll(zj, 1, 1) * sin_odd)
    return outs[0] if len(outs) == 1 else jnp.concatenate(outs, axis=1)


def _augment_values_t(v_t):
    ones = jnp.ones((ONES_ROWS, v_t.shape[1]), v_t.dtype)
    parts = []
    for kv in range(N_KV_HEADS):
        parts += [v_t[kv * HEAD_DIM:(kv + 1) * HEAD_DIM], ones]
    return jnp.concatenate(parts, axis=0).astype(BF16)


def _inproj_kernel(*refs, rope):
    if rope:
        (x_ref, sh_ref, sc_ref, g_ref, w_ref, wvt_ref, cos_ref, se_ref, so_ref,
         uf_ref, q_ref, k_ref, v_ref, gate_ref) = refs
        rot = lambda z, rows: _rope(z, cos_ref[rows, :], se_ref[rows, :], so_ref[rows, :])
    else:
        x_ref, sh_ref, sc_ref, g_ref, w_ref, _, _, uf_ref, q_ref, k_ref, v_ref, gate_ref = refs
        rot = lambda z, rows: z
    for r0 in range(0, x_ref.shape[0], SUB_TILE):
        rows = slice(r0, r0 + SUB_TILE)
        h = _rmsnorm(x_ref[rows, :], g_ref[...]) * (1.0 + sc_ref[...]) + sh_ref[...]
        h = h.astype(BF16)

        def seg(start, h=h):
            return _dot(h, w_ref[:, start:start + COL_CHUNK])

        uf_ref[rows, :] = seg(0).astype(uf_ref.dtype)
        for j in range(ATTN_WIDTH // COL_CHUNK):
            z = rot(seg(OFF_Q + j * COL_CHUNK), rows) * QUERY_SCALE
            q_ref[rows, j * COL_CHUNK:(j + 1) * COL_CHUNK] = z.astype(q_ref.dtype)
        if rope:
            k_ref[rows, :] = rot(_dot(h, w_ref[:, OFF_K:OFF_V]), rows).astype(k_ref.dtype)
            v_t = lax.dot_general(wvt_ref[...], h, (((1,), (1,)), ((), ())), preferred_element_type=F32)
            v_ref[:, rows] = _augment_values_t(v_t)
        else:
            z = seg(OFF_K)
            seqs = slice(r0 // k_ref.shape[1], (r0 + SUB_TILE) // k_ref.shape[1])
            k_ref[seqs] = z[:, :KV_WIDTH].reshape(k_ref[seqs].shape)
            v_ref[seqs] = z[:, KV_WIDTH:].reshape(v_ref[seqs].shape)
        for j in range(2 * D_MODEL // COL_CHUNK):
            z = seg(OFF_G + j * COL_CHUNK)
            gate_ref[rows, j * COL_CHUNK:(j + 1) * COL_CHUNK] = _sigmoid(z).astype(gate_ref.dtype)


def _in_projection(x, mod, layer, row_of_tile, g1, w_in, tm, *, w_v_t=None, rope_tabs=None, kv_cache=None):
    rows = x.shape[0]
    tile = lambda w: pl.BlockSpec((tm, w), lambda i: (i, 0))
    in_specs = [tile(D_MODEL), _mod_spec(layer, 0, row_of_tile), _mod_spec(layer, 1, row_of_tile),
                _resident((1, D_MODEL), layer), _resident((D_MODEL, IN_WIDTH), layer)]
    args = [x, mod, mod, g1, w_in]
    out_specs = [tile(FOURIER_WIDTH), tile(ATTN_WIDTH), None, None, tile(2 * D_MODEL)]
    out_shape = [jax.ShapeDtypeStruct((rows, FOURIER_WIDTH), BF16), jax.ShapeDtypeStruct((rows, ATTN_WIDTH), BF16),
                 None, None, jax.ShapeDtypeStruct((rows, 2 * D_MODEL), BF16)]
    latent = rope_tabs is not None
    aliases = {}
    if latent:
        tiles_per_seq = rope_tabs[0].shape[0] // tm
        in_specs += [_resident((KV_WIDTH, D_MODEL), layer)]
        in_specs += [pl.BlockSpec((tm, LANES), lambda i: (i % tiles_per_seq, 0))] * 3
        args += [w_v_t] + list(rope_tabs)
        out_specs[2] = tile(KV_WIDTH)
        out_shape[2] = jax.ShapeDtypeStruct((rows, KV_WIDTH), BF16)
        out_specs[3] = pl.BlockSpec((N_KV_HEADS * AUG_ROWS, tm), lambda i: (0, i))
        out_shape[3] = jax.ShapeDtypeStruct((N_KV_HEADS * AUG_ROWS, rows), BF16)
    else:
        s_ctx = kv_cache[0].shape[2]
        assert tm % s_ctx == 0
        in_specs += [pl.BlockSpec(memory_space=pl.ANY)] * 2
        args += list(kv_cache)
        aliases = {5: 2, 6: 3}
        for slot in (2, 3):
            out_specs[slot] = pl.BlockSpec((tm // s_ctx, None, s_ctx, KV_WIDTH), lambda i: (i, layer, 0, 0))
            out_shape[slot] = jax.ShapeDtypeStruct(kv_cache[0].shape, F32)
    return pl.pallas_call(
        functools.partial(_inproj_kernel, rope=latent),
        grid=(rows // tm,),
        in_specs=in_specs,
        out_specs=out_specs,
        out_shape=out_shape,
        input_output_aliases=aliases,
        compiler_params=_params(1),
        name="in_projection",
    )(*args)


def _softmax_head(s_ref, p_ref, cols, sink):
    n_keys = s_ref.shape[0]
    blocks = [s_ref[r:r + MAX_ROWS, cols] for r in range(0, n_keys, MAX_ROWS)]
    m = jnp.maximum(jnp.max(functools.reduce(jnp.maximum, blocks), axis=0, keepdims=True), sink)
    for r in range(0, n_keys, EXP_ROWS):
        rows = slice(r, r + EXP_ROWS)
        p_ref[rows, cols] = jnp.exp2(s_ref[rows, cols] - m).astype(BF16)
    return jnp.exp2(sink - m)


def _attend(q_of, key_parts, bias_t_of, n_biased, sink_of, s_ref, p_ref, o_ref):
    t = o_ref.shape[0]

    def score(kv):
        qs = jnp.concatenate([q_of(kv * Q_PER_KV + g) for g in range(Q_PER_KV)], axis=0)
        row = 0
        for k_of, _ in key_parts:
            k = k_of(kv)
            part = lax.dot_general(k, qs, (((1,), (1,)), ((), ())), preferred_element_type=F32)
            if row < n_biased:
                part = part + jnp.concatenate([bias_t_of(slice(row, row + k.shape[0]))] * Q_PER_KV, axis=1)
            s_ref[kv, row:row + k.shape[0], :] = part
            row += k.shape[0]

    score(0)
    for kv in range(N_KV_HEADS):
        if kv + 1 < N_KV_HEADS:
            score(kv + 1)
        sink_terms = []
        for g in range(Q_PER_KV):
            sink = sink_of(kv * Q_PER_KV + g) * LOG2E
            sink_terms.append(_softmax_head(s_ref.at[kv], p_ref.at[kv], slice(g * t, (g + 1) * t), sink))
        o_aug, row = None, 0
        for _, v_t_of in key_parts:
            v_t = v_t_of(kv)
            part = _dot(v_t, p_ref[kv, row:row + v_t.shape[1], :])
            o_aug = part if o_aug is None else o_aug + part
            row += v_t.shape[1]
        denom = o_aug[HEAD_DIM:HEAD_DIM + 1, :] + jnp.concatenate(sink_terms, axis=1)
        o_t = o_aug[:HEAD_DIM, :] / denom
        for pair in range(Q_PER_KV // 2):
            two = jnp.concatenate([o_t[:, (2 * pair) * t:(2 * pair + 1) * t],
                                   o_t[:, (2 * pair + 1) * t:(2 * pair + 2) * t]], axis=0)
            col = (kv * (Q_PER_KV // 2) + pair) * LANES
            o_ref[:, col:col + LANES] = two.T.astype(o_ref.dtype)


def _head_cols(head):
    return slice(head * HEAD_DIM, (head + 1) * HEAD_DIM)


def _attn_ctx_kernel(sink_ref, q_ref, k_ref, v_ref, o_ref, s_ref, p_ref, *, layer):
    for b in range(q_ref.shape[0]):
        k = k_ref[b].astype(BF16)
        v_t = _augment_values_t(v_ref[b].T)
        _attend(lambda h: q_ref[b, :, _head_cols(h)],
                [(lambda kv: k[:, _head_cols(kv)], lambda kv: v_t[kv * AUG_ROWS:(kv + 1) * AUG_ROWS])],
                None, 0, lambda h: sink_ref[layer, h], s_ref.at[b % SCRATCH_SETS], p_ref.at[b % SCRATCH_SETS], o_ref.at[b])


def _context_attention(q, k, v, sink, layer):
    b, s, _ = q.shape
    nb = CTX_SEQS_PER_STEP
    assert s % MAX_ROWS == 0 and b % nb == 0
    blk = pl.BlockSpec((nb, s, ATTN_WIDTH), lambda i: (i, 0, 0))
    cache = pl.BlockSpec((nb, None, s, KV_WIDTH), lambda i: (i, layer, 0, 0))
    return pl.pallas_call(
        functools.partial(_attn_ctx_kernel, layer=layer),
        grid=(b // nb,),
        in_specs=[pl.BlockSpec(memory_space=pltpu.SMEM), blk, cache, cache],
        out_specs=blk,
        out_shape=jax.ShapeDtypeStruct((b, s, ATTN_WIDTH), BF16),
        scratch_shapes=[pltpu.VMEM((SCRATCH_SETS, N_KV_HEADS, s, Q_PER_KV * s), F32),
                        pltpu.VMEM((SCRATCH_SETS, N_KV_HEADS, s, Q_PER_KV * s), BF16)],
        compiler_params=_params(1),
        name="context_attention",
    )(sink, q, k, v)


def _window_bias_t():
    c = np.arange(KEY_SPAN)[None, :, None]
    r = np.arange(Q_BLOCK)[None, None, :]
    p = np.arange(KEY_SPAN // Q_BLOCK)[:, None, None]
    return jnp.asarray(np.where(np.abs(c - (r + p * Q_BLOCK)) <= WINDOW, 0.0, NEG_INF).astype(np.float32))


def _attn_lat_kernel(sink_ref, q_ref, k_ref, vt_ref, kc_ref, vc_ref, bias_ref, o_ref, kcb_ref, vcb_ref, s_ref, p_ref,
                     *, layer):
    t = Q_BLOCK
    s_len = k_ref.shape[0]

    @pl.when(pl.program_id(1) == 0)
    def _():
        kcb_ref[...] = kc_ref[...].astype(BF16)
        vcb_ref[...] = _augment_values_t(vc_ref[...].T)

    context = (lambda kv: kcb_ref[:, _head_cols(kv)], lambda kv: vcb_ref[kv * AUG_ROWS:(kv + 1) * AUG_ROWS, :])
    for j in range(q_ref.shape[0] // t):
        i = pl.program_id(1) * (q_ref.shape[0] // t) + j
        start = pl.multiple_of(jnp.clip((i - 1) * t, 0, s_len - KEY_SPAN), t)
        pattern = lax.div(i * t - start, t)
        window = (lambda kv, start=start: k_ref[pl.ds(start, KEY_SPAN), _head_cols(kv)],
                  lambda kv, start=start: vt_ref[kv * AUG_ROWS:(kv + 1) * AUG_ROWS, pl.ds(start, KEY_SPAN)])
        qrows = slice(j * t, (j + 1) * t)
        _attend(lambda h, qrows=qrows: q_ref[qrows, _head_cols(h)], [window, context],
                lambda rows, pattern=pattern: bias_ref[pattern, rows, :], KEY_SPAN,
                lambda h: sink_ref[layer, h], s_ref.at[j % SCRATCH_SETS], p_ref.at[j % SCRATCH_SETS], o_ref.at[qrows])


def _latent_attention(q, k, v_t, cache_k, cache_v, sink, layer):
    b, s, _ = q.shape
    past = cache_k.shape[2]
    nq = LATENT_QBLOCKS_PER_STEP
    assert s % (nq * Q_BLOCK) == 0 and s >= KEY_SPAN and WINDOW <= Q_BLOCK and Q_BLOCK == LANES
    assert past % MAX_ROWS == 0
    n_keys = KEY_SPAN + past
    qblk = pl.BlockSpec((None, nq * Q_BLOCK, ATTN_WIDTH), lambda bi, i: (bi, i, 0))
    cache = pl.BlockSpec((None, None, past, KV_WIDTH), lambda bi, i: (bi, layer, 0, 0))
    bias = _window_bias_t()
    return pl.pallas_call(
        functools.partial(_attn_lat_kernel, layer=layer),
        grid=(b, s // (nq * Q_BLOCK)),
        in_specs=[pl.BlockSpec(memory_space=pltpu.SMEM), qblk,
                  pl.BlockSpec((None, s, KV_WIDTH), lambda bi, i: (bi, 0, 0)),
                  pl.BlockSpec((N_KV_HEADS * AUG_ROWS, s), lambda bi, i: (0, bi)),
                  cache, cache, _resident(bias.shape)],
        out_specs=qblk,
        out_shape=jax.ShapeDtypeStruct((b, s, ATTN_WIDTH), BF16),
        scratch_shapes=[pltpu.VMEM((past, KV_WIDTH), BF16), pltpu.VMEM((N_KV_HEADS * AUG_ROWS, past), BF16),
                        pltpu.VMEM((SCRATCH_SETS, N_KV_HEADS, n_keys, Q_PER_KV * Q_BLOCK), F32),
                        pltpu.VMEM((SCRATCH_SETS, N_KV_HEADS, n_keys, Q_PER_KV * Q_BLOCK), BF16)],
        compiler_params=_params(2),
        name="latent_attention",
    )(sink, q, k, v_t, cache_k, cache_v, bias)


def _dft_tables(s):
    def cos_sin(n, rows):
        k = np.arange(rows, dtype=np.int64)[:, None]
        j = np.arange(n, dtype=np.int64)[None, :]
        ang = 2.0 * np.pi * ((k * j) % n) / n
        scale = n ** -0.5
        return (np.cos(ang) * scale).astype(np.float32), (np.sin(ang) * scale).astype(np.float32)
    as_bf16 = lambda t: jnp.asarray(t).astype(BF16)
    cc, sc = cos_sin(FOURIER_GROUP_DIM, FOURIER_GROUP_DIM)
    chan = as_bf16(np.concatenate([cc, sc], axis=1))
    if s < SYMMETRIC_MIN_SEQ:
        cs, ss = cos_sin(s, s)
        return chan, as_bf16(cs), as_bf16(ss)
    half = s // 2
    cs, ss = cos_sin(s, half + TABLE_PAD)
    rev = np.zeros((half, half), np.float32)
    rev[np.arange(1, half), half - np.arange(1, half)] = 1.0
    return chan, as_bf16(cs), as_bf16(ss[:half]), as_bf16(rev)


def _fourier_kernel(x_ref, chan_ref, cos_ref, sin_ref, *rest, row_chunk):
    *rev_ref, o_ref, ab_ref, eo_ref = rest
    nb, s, _ = x_ref.shape
    n_direct = sin_ref.shape[0]
    gd = FOURIER_GROUP_DIM
    for b in range(nb):
        for r in range(0, s, row_chunk):
            for g in range(N_FOURIER_GROUPS):
                ab = _dot(x_ref[b, r:r + row_chunk, g * gd:(g + 1) * gd], chan_ref[...])
                ab_ref[r:r + row_chunk, g * gd:(g + 1) * gd] = ab[:, :gd].astype(BF16)
                ab_ref[s + r:s + r + row_chunk, g * gd:(g + 1) * gd] = ab[:, gd:].astype(BF16)
        for r in range(0, n_direct, row_chunk):
            rows = slice(r, r + row_chunk)
            e = _dot(cos_ref[rows, :], ab_ref[:s, :])
            o = _dot(sin_ref[rows, :], ab_ref[s:, :])
            o_ref[b, rows, :] = (e - o).astype(o_ref.dtype)
            if rev_ref:
                eo_ref[rows, :] = (e + o).astype(BF16)
        if rev_ref:
            e_mid = _dot(cos_ref[n_direct:n_direct + TABLE_PAD, :], ab_ref[:s, :])
            first = lax.broadcasted_iota(jnp.int32, e_mid.shape, 0) == 0
            for r in range(0, n_direct, row_chunk):
                up = _dot(rev_ref[0][r:r + row_chunk, :], eo_ref[...])
                if r == 0:
                    up = jnp.concatenate([up[:TABLE_PAD] + jnp.where(first, e_mid, 0.0), up[TABLE_PAD:]], axis=0)
                o_ref[b, n_direct + r:n_direct + r + row_chunk, :] = up.astype(o_ref.dtype)


def _fourier_mix(u, tables, seqs_per_step):
    b, s, w = u.shape
    n_direct = tables[2].shape[0]
    row_chunk = min(n_direct, 512)
    blk = pl.BlockSpec((seqs_per_step, s, w), lambda i: (i, 0, 0))
    return pl.pallas_call(
        functools.partial(_fourier_kernel, row_chunk=row_chunk),
        grid=(b // seqs_per_step,),
        in_specs=[blk] + [_resident(t.shape) for t in tables],
        out_specs=blk,
        out_shape=jax.ShapeDtypeStruct((b, s, w), BF16),
        scratch_shapes=[pltpu.VMEM((2 * s, w), BF16), pltpu.VMEM((n_direct, w), BF16)],
        compiler_params=_params(1),
        name="fourier_mix",
    )(u, *tables)


def _merge_mlp_kernel(*refs, final):
    (x_ref, f_ref, a_ref, gate_ref, ga1_ref, sh2_ref, sc2_ref, ga2_ref, g2_ref,
     wfo_ref, wao_ref, wout_ref, w1_ref, w2_ref) = refs[:14]
    o_ref = refs[-1]
    gate_f = gate_ref[:, :D_MODEL].astype(F32)
    gate_a = gate_ref[:, D_MODEL:].astype(F32)
    m = gate_f * _dot(f_ref[...], wfo_ref[...]) + gate_a * _dot(a_ref[...], wao_ref[...])
    x1 = x_ref[...] + ga1_ref[...] * _dot(m.astype(BF16), wout_ref[...])
    h = (_rmsnorm(x1, g2_ref[...]) * (1.0 + sc2_ref[...]) + sh2_ref[...]).astype(BF16)
    ff = None
    for j in range(D_FF // COL_CHUNK):
        cs = slice(j * COL_CHUNK, (j + 1) * COL_CHUNK)
        t = jnp.maximum(_dot(h, w1_ref[:, cs]), 0.0)
        part = _dot((t * t).astype(BF16), w2_ref[cs, :])
        ff = part if ff is None else ff + part
    x2 = x1 + ga2_ref[...] * ff
    if final:
        x2 = _rmsnorm(x2, refs[14][...])
    o_ref[...] = x2


def _merge_mlp(x, f, attn, gates, mod, layer, row_of_tile, g2, w_fo, w_ao, w_out, w_ff1, w_ff2, final_g, tm):
    rows = x.shape[0]
    tile = lambda w: pl.BlockSpec((tm, w), lambda i: (i, 0))
    in_specs = [tile(D_MODEL), tile(FOURIER_WIDTH), tile(ATTN_WIDTH), tile(2 * D_MODEL)]
    in_specs += [_mod_spec(layer, slot, row_of_tile) for slot in (2, 3, 4, 5)]
    in_specs += [_resident((1, D_MODEL), layer)]
    in_specs += [_resident(w.shape[1:], layer) for w in (w_fo, w_ao, w_out, w_ff1, w_ff2)]
    args = [x, f, attn, gates, mod, mod, mod, mod, g2, w_fo, w_ao, w_out, w_ff1, w_ff2]
    if final_g is not None:
        in_specs.append(_resident((1, D_MODEL)))
        args.append(final_g)
    return pl.pallas_call(
        functools.partial(_merge_mlp_kernel, final=final_g is not None),
        grid=(rows // tm,),
        in_specs=in_specs,
        out_specs=tile(D_MODEL),
        out_shape=jax.ShapeDtypeStruct((rows, D_MODEL), F32),
        compiler_params=_params(1),
        name="merge_mlp",
    )(*args)


def _rope_tables(n_tokens):
    t = jnp.arange(n_tokens)
    row = (t // GRID_W).astype(F32)
    col = (t % GRID_W).astype(F32)
    inv = ROPE_THETA ** (-jnp.arange(ROPE_PAIRS_PER_AXIS, dtype=F32) / ROPE_PAIRS_PER_AXIS)
    ang = jnp.concatenate([row[:, None] * inv[None, :], col[:, None] * inv[None, :]], axis=-1)
    cos = jnp.repeat(jnp.cos(ang), 2, axis=1)
    sin = jnp.repeat(jnp.sin(ang), 2, axis=1)
    even = (jnp.arange(HEAD_DIM) % 2 == 0)[None, :]
    reps = LANES // HEAD_DIM
    return (jnp.tile(cos, (1, reps)),
            jnp.tile(jnp.where(even, -sin, 0.0), (1, reps)),
            jnp.tile(jnp.where(even, 0.0, sin), (1, reps)))


def kernel(x_prompt, x_sample, c, cache_k, cache_v, c_ctx, w_ada, b_ada, norm1_g, norm2_g,
           w_in, sink, w_fo, w_ao, w_out, w_ff1, w_ff2, final_g):
    depth = w_in.shape[0]
    b_ctx, s_ctx, _ = x_prompt.shape
    b_lat, s_lat, _ = x_sample.shape
    past = cache_k.shape[2]
    assert 1 + b_lat <= MOD_ROWS
    tm, tm_in = ROW_TILE, INPROJ_ROW_TILE
    assert (b_ctx * s_ctx) % tm == 0 and s_lat % tm == 0 and (b_ctx * s_ctx) % tm_in == 0 and s_lat % tm_in == 0
    assert tm_in % SUB_TILE == 0 and SUB_TILE % s_ctx == 0

    cvecs = jnp.zeros((MOD_ROWS, D_MODEL), F32).at[0].set(c_ctx).at[1:1 + b_lat].set(c)
    mod = _modulation(cvecs, w_ada, b_ada).reshape(depth, MOD_ROWS, 6, 1, D_MODEL)
    ctx_row = lambda tile: lambda i: 0
    lat_row = lambda tile: lambda i: 1 + i // (s_lat // tile)

    w_in_b, w_fo_b, w_ao_b, w_out_b, w_ff1_b, w_ff2_b = (
        w.astype(BF16) for w in (w_in, w_fo, w_ao, w_out, w_ff1, w_ff2))
    w_v_t = jnp.swapaxes(w_in_b[:, :, OFF_V:OFF_G], 1, 2)
    g1 = norm1_g.reshape(depth, 1, D_MODEL)
    g2 = norm2_g.reshape(depth, 1, D_MODEL)
    gf = final_g.reshape(1, D_MODEL)
    rope_tabs = _rope_tables(s_lat)
    dft_ctx = _dft_tables(s_ctx)
    dft_lat = _dft_tables(s_lat)
    cache_k = cache_k.reshape(b_lat, depth, past, KV_WIDTH)
    cache_v = cache_v.reshape(b_lat, depth, past, KV_WIDTH)

    def layer_tail(x, uf, attn, gates, l, row_of_tile, fourier_tabs, batch, seqs_per_step):
        seq = x.shape[0] // batch
        f = _fourier_mix(uf.reshape(batch, seq, FOURIER_WIDTH), fourier_tabs, seqs_per_step)
        return _merge_mlp(x, f.reshape(-1, FOURIER_WIDTH), attn.reshape(-1, ATTN_WIDTH), gates, mod, l, row_of_tile,
                          g2, w_fo_b, w_ao_b, w_out_b, w_ff1_b, w_ff2_b, gf if l == depth - 1 else None, tm)

    xc = x_prompt.reshape(b_ctx * s_ctx, D_MODEL)
    kv_cache = (jnp.zeros((b_ctx, depth, s_ctx, KV_WIDTH), F32),) * 2
    for l in range(depth):
        uf, q, *kv_cache, gates = _in_projection(xc, mod, l, ctx_row(tm_in), g1, w_in_b, tm_in, kv_cache=kv_cache)
        attn = _context_attention(q.reshape(b_ctx, s_ctx, ATTN_WIDTH), *kv_cache, sink, l)
        xc = layer_tail(xc, uf, attn, gates, l, ctx_row(tm), dft_ctx, b_ctx, 4 if b_ctx % 4 == 0 else 1)
    y_prompt = xc.reshape(b_ctx, s_ctx, D_MODEL)
    new_k, new_v = (buf.reshape(b_ctx, depth, s_ctx, N_KV_HEADS, HEAD_DIM) for buf in kv_cache)

    xs = x_sample.reshape(b_lat * s_lat, D_MODEL)
    for l in range(depth):
        uf, q, k, v_t, gates = _in_projection(xs, mod, l, lat_row(tm_in), g1, w_in_b, tm_in, w_v_t=w_v_t,
                                              rope_tabs=rope_tabs)
        attn = _latent_attention(q.reshape(b_lat, s_lat, ATTN_WIDTH), k.reshape(b_lat, s_lat, KV_WIDTH),
                                 v_t, cache_k, cache_v, sink, l)
        xs = layer_tail(xs, uf, attn, gates, l, lat_row(tm), dft_lat, b_lat, 1)
    y_sample = xs.reshape(b_lat, s_lat, D_MODEL)

    return y_prompt, y_sample, new_k, new_v
```

```python
import functools

import numpy as np
import jax
import jax.numpy as jnp
from jax import lax
from jax.experimental import pallas as pl
from jax.experimental.pallas import tpu as pltpu

D_MODEL = 1024
N_HEADS = 16
N_KV_HEADS = 4
HEAD_DIM = 64
Q_PER_KV = N_HEADS // N_KV_HEADS
ATTN_WIDTH = N_HEADS * HEAD_DIM
KV_WIDTH = N_KV_HEADS * HEAD_DIM
N_FOURIER_GROUPS = 4
FOURIER_GROUP_DIM = 128
FOURIER_WIDTH = N_FOURIER_GROUPS * FOURIER_GROUP_DIM
IN_WIDTH = FOURIER_WIDTH + ATTN_WIDTH + 2 * KV_WIDTH + 2 * D_MODEL
D_FF = 4 * D_MODEL
GRID_W = 64
WINDOW = 128
ROPE_THETA = 10000.0
ROPE_PAIRS_PER_AXIS = HEAD_DIM // 4
EPS = 1e-6
NEG_INF = -1e30
LOG2E = 1.4426950408889634
QUERY_SCALE = HEAD_DIM ** -0.5 * LOG2E

OFF_Q = FOURIER_WIDTH
OFF_K = OFF_Q + ATTN_WIDTH
OFF_V = OFF_K + KV_WIDTH
OFF_G = OFF_V + KV_WIDTH

LANES = 128
MOD_ROWS = 16
VMEM_LIMIT_BYTES = 56 * 1024 * 1024
ROW_TILE = 512
INPROJ_ROW_TILE = 1024
SUB_TILE = 512
COL_CHUNK = 512
Q_BLOCK = 128
KEY_SPAN = 3 * Q_BLOCK
MAX_ROWS = 128
EXP_ROWS = 64
LATENT_QBLOCKS_PER_STEP = 16
CTX_SEQS_PER_STEP = 8
SCRATCH_SETS = 2
SYMMETRIC_MIN_SEQ = 1024
TABLE_PAD = 16
ONES_ROWS = 16
AUG_ROWS = HEAD_DIM + ONES_ROWS

BF16 = jnp.bfloat16
F32 = jnp.float32


def _params(n_axes):
    return pltpu.CompilerParams(dimension_semantics=("arbitrary",) * n_axes,
                                vmem_limit_bytes=VMEM_LIMIT_BYTES)


def _resident(shape, layer=None):
    zeros = (0,) * len(shape)
    if layer is None:
        return pl.BlockSpec(shape, lambda *_: zeros, pipeline_mode=pl.Buffered(1))
    return pl.BlockSpec((None,) + tuple(shape), lambda *_: (layer,) + zeros, pipeline_mode=pl.Buffered(1))


def _dot(a, b):
    return jnp.dot(a, b, preferred_element_type=F32)


def _sigmoid(z):
    return 1.0 / (1.0 + jnp.exp(-z))


def _rmsnorm(x, g):
    return x * lax.rsqrt(jnp.mean(x * x, axis=-1, keepdims=True) + EPS) * g


def _mod_kernel(c_ref, w_ref, b_ref, o_ref):
    c = c_ref[...]
    a = c * _sigmoid(c)
    a_hi = a.astype(BF16)
    a_lo = (a - a_hi.astype(F32)).astype(BF16)
    w = w_ref[...]
    w_hi = w.astype(BF16)
    w_lo = (w - w_hi.astype(F32)).astype(BF16)
    o_ref[...] = _dot(a_hi, w_hi) + _dot(a_lo, w_hi) + _dot(a_hi, w_lo) + b_ref[...]


def _modulation(cvecs, w_ada, b_ada):
    depth, _, width = w_ada.shape
    tn = width // 4
    return pl.pallas_call(
        _mod_kernel,
        grid=(depth, width // tn),
        in_specs=[
            pl.BlockSpec((MOD_ROWS, D_MODEL), lambda l, j: (0, 0)),
            pl.BlockSpec((None, D_MODEL, tn), lambda l, j: (l, 0, j)),
            pl.BlockSpec((None, 1, tn), lambda l, j: (l, 0, j)),
        ],
        out_specs=pl.BlockSpec((None, MOD_ROWS, tn), lambda l, j: (l, 0, j)),
        out_shape=jax.ShapeDtypeStruct((depth, MOD_ROWS, width), F32),
        compiler_params=_params(2),
        name="modulation",
    )(cvecs, w_ada, b_ada.reshape(depth, 1, width))


def _mod_spec(layer, slot, row_of_tile):
    return pl.BlockSpec((None, None, None, 1, D_MODEL),
                        lambda i: (layer, row_of_tile(i), slot, 0, 0))


def _rope(z, cos, sin_even, sin_odd):
    outs = []
    for j in range(z.shape[1] // LANES):
        zj = z[:, j * LANES:(j + 1) * LANES]
        outs.append(zj * cos + pltpu.roll(zj, LANES - 1, 1) * sin_even + pltpu.roll(zj, 1, 1) * sin_odd)
    return outs[0] if len(outs) == 1 else jnp.concatenate(outs, axis=1)


def _augment_values_t(v_t):
    ones = jnp.ones((ONES_ROWS, v_t.shape[1]), v_t.dtype)
    parts = []
    for kv in range(N_KV_HEADS):
        parts += [v_t[kv * HEAD_DIM:(kv + 1) * HEAD_DIM], ones]
    return jnp.concatenate(parts, axis=0).astype(BF16)


def _inproj_kernel(*refs, rope):
    if rope:
        (x_ref, sh_ref, sc_ref, g_ref, w_ref, wvt_ref, cos_ref, se_ref, so_ref,
         uf_ref, q_ref, k_ref, v_ref, gate_ref) = refs
        rot = lambda z, rows: _rope(z, cos_ref[rows, :], se_ref[rows, :], so_ref[rows, :])
    else:
        x_ref, sh_ref, sc_ref, g_ref, w_ref, _, _, uf_ref, q_ref, k_ref, v_ref, gate_ref = refs
        rot = lambda z, rows: z
    for r0 in range(0, x_ref.shape[0], SUB_TILE):
        rows = slice(r0, r0 + SUB_TILE)
        h = _rmsnorm(x_ref[rows, :], g_ref[...]) * (1.0 + sc_ref[...]) + sh_ref[...]
        h = h.astype(BF16)

        def seg(start, h=h):
            return _dot(h, w_ref[:, start:start + COL_CHUNK])

        uf_ref[rows, :] = seg(0).astype(uf_ref.dtype)
        for j in range(ATTN_WIDTH // COL_CHUNK):
            z = rot(seg(OFF_Q + j * COL_CHUNK), rows) * QUERY_SCALE
            q_ref[rows, j * COL_CHUNK:(j + 1) * COL_CHUNK] = z.astype(q_ref.dtype)
        if rope:
            k_ref[rows, :] = rot(_dot(h, w_ref[:, OFF_K:OFF_V]), rows).astype(k_ref.dtype)
            v_t = lax.dot_general(wvt_ref[...], h, (((1,), (1,)), ((), ())), preferred_element_type=F32)
            v_ref[:, rows] = _augment_values_t(v_t)
        else:
            z = seg(OFF_K)
            seqs = slice(r0 // k_ref.shape[1], (r0 + SUB_TILE) // k_ref.shape[1])
            k_ref[seqs] = z[:, :KV_WIDTH].reshape(k_ref[seqs].shape)
            v_ref[seqs] = z[:, KV_WIDTH:].reshape(v_ref[seqs].shape)
        for j in range(2 * D_MODEL // COL_CHUNK):
            z = seg(OFF_G + j * COL_CHUNK)
            gate_ref[rows, j * COL_CHUNK:(j + 1) * COL_CHUNK] = _sigmoid(z).astype(gate_ref.dtype)


def _in_projection(x, mod, layer, row_of_tile, g1, w_in, tm, *, w_v_t=None, rope_tabs=None, kv_cache=None):
    rows = x.shape[0]
    tile = lambda w: pl.BlockSpec((tm, w), lambda i: (i, 0))
    in_specs = [tile(D_MODEL), _mod_spec(layer, 0, row_of_tile), _mod_spec(layer, 1, row_of_tile),
                _resident((1, D_MODEL), layer), _resident((D_MODEL, IN_WIDTH), layer)]
    args = [x, mod, mod, g1, w_in]
    out_specs = [tile(FOURIER_WIDTH), tile(ATTN_WIDTH), None, None, tile(2 * D_MODEL)]
    out_shape = [jax.ShapeDtypeStruct((rows, FOURIER_WIDTH), BF16), jax.ShapeDtypeStruct((rows, ATTN_WIDTH), BF16),
                 None, None, jax.ShapeDtypeStruct((rows, 2 * D_MODEL), BF16)]
    latent = rope_tabs is not None
    aliases = {}
    if latent:
        tiles_per_seq = rope_tabs[0].shape[0] // tm
        in_specs += [_resident((KV_WIDTH, D_MODEL), layer)]
        in_specs += [pl.BlockSpec((tm, LANES), lambda i: (i % tiles_per_seq, 0))] * 3
        args += [w_v_t] + list(rope_tabs)
        out_specs[2] = tile(KV_WIDTH)
        out_shape[2] = jax.ShapeDtypeStruct((rows, KV_WIDTH), BF16)
        out_specs[3] = pl.BlockSpec((N_KV_HEADS * AUG_ROWS, tm), lambda i: (0, i))
        out_shape[3] = jax.ShapeDtypeStruct((N_KV_HEADS * AUG_ROWS, rows), BF16)
    else:
        s_ctx = kv_cache[0].shape[2]
        assert tm % s_ctx == 0
        in_specs += [pl.BlockSpec(memory_space=pl.ANY)] * 2
        args += list(kv_cache)
        aliases = {5: 2, 6: 3}
        for slot in (2, 3):
            out_specs[slot] = pl.BlockSpec((tm // s_ctx, None, s_ctx, KV_WIDTH), lambda i: (i, layer, 0, 0))
            out_shape[slot] = jax.ShapeDtypeStruct(kv_cache[0].shape, F32)
    return pl.pallas_call(
        functools.partial(_inproj_kernel, rope=latent),
        grid=(rows // tm,),
        in_specs=in_specs,
        out_specs=out_specs,
        out_shape=out_shape,
        input_output_aliases=aliases,
        compiler_params=_params(1),
        name="in_projection",
    )(*args)


def _softmax_head(s_ref, p_ref, cols, sink):
    n_keys = s_ref.shape[0]
    blocks = [s_ref[r:r + MAX_ROWS, cols] for r in range(0, n_keys, MAX_ROWS)]
    m = jnp.maximum(jnp.max(functools.reduce(jnp.maximum, blocks), axis=0, keepdims=True), sink)
    for r in range(0, n_keys, EXP_ROWS):
        rows = slice(r, r + EXP_ROWS)
        p_ref[rows, cols] = jnp.exp2(s_ref[rows, cols] - m).astype(BF16)
    return jnp.exp2(sink - m)


def _attend(q_of, key_parts, bias_t_of, n_biased, sink_of, s_ref, p_ref, o_ref):
    t = o_ref.shape[0]

    def score(kv):
        qs = jnp.concatenate([q_of(kv * Q_PER_KV + g) for g in range(Q_PER_KV)], axis=0)
        row = 0
        for k_of, _ in key_parts:
            k = k_of(kv)
            part = lax.dot_general(k, qs, (((1,), (1,)), ((), ())), preferred_element_type=F32)
            if row < n_biased:
                part = part + jnp.concatenate([bias_t_of(slice(row, row + k.shape[0]))] * Q_PER_KV, axis=1)
            s_ref[kv, row:row + k.shape[0], :] = part
            row += k.shape[0]

    score(0)
    for kv in range(N_KV_HEADS):
        if kv + 1 < N_KV_HEADS:
            score(kv + 1)
        sink_terms = []
        for g in range(Q_PER_KV):
            sink = sink_of(kv * Q_PER_KV + g) * LOG2E
            sink_terms.append(_softmax_head(s_ref.at[kv], p_ref.at[kv], slice(g * t, (g + 1) * t), sink))
        o_aug, row = None, 0
        for _, v_t_of in key_parts:
            v_t = v_t_of(kv)
            part = _dot(v_t, p_ref[kv, row:row + v_t.shape[1], :])
            o_aug = part if o_aug is None else o_aug + part
            row += v_t.shape[1]
        denom = o_aug[HEAD_DIM:HEAD_DIM + 1, :] + jnp.concatenate(sink_terms, axis=1)
        o_t = o_aug[:HEAD_DIM, :] / denom
        for pair in range(Q_PER_KV // 2):
            two = jnp.concatenate([o_t[:, (2 * pair) * t:(2 * pair + 1) * t],
                                   o_t[:, (2 * pair + 1) * t:(2 * pair + 2) * t]], axis=0)
            col = (kv * (Q_PER_KV // 2) + pair) * LANES
            o_ref[:, col:col + LANES] = two.T.astype(o_ref.dtype)


def _head_cols(head):
    return slice(head * HEAD_DIM, (head + 1) * HEAD_DIM)


def _attn_ctx_kernel(sink_ref, q_ref, k_ref, v_ref, o_ref, s_ref, p_ref, *, layer):
    for b in range(q_ref.shape[0]):
        k = k_ref[b].astype(BF16)
        v_t = _augment_values_t(v_ref[b].T)
        _attend(lambda h: q_ref[b, :, _head_cols(h)],
                [(lambda kv: k[:, _head_cols(kv)], lambda kv: v_t[kv * AUG_ROWS:(kv + 1) * AUG_ROWS])],
                None, 0, lambda h: sink_ref[layer, h], s_ref.at[b % SCRATCH_SETS], p_ref.at[b % SCRATCH_SETS], o_ref.at[b])


def _context_attention(q, k, v, sink, layer):
    b, s, _ = q.shape
    nb = CTX_SEQS_PER_STEP
    assert s % MAX_ROWS == 0 and b % nb == 0
    blk = pl.BlockSpec((nb, s, ATTN_WIDTH), lambda i: (i, 0, 0))
    cache = pl.BlockSpec((nb, None, s, KV_WIDTH), lambda i: (i, layer, 0, 0))
    return pl.pallas_call(
        functools.partial(_attn_ctx_kernel, layer=layer),
        grid=(b // nb,),
        in_specs=[pl.BlockSpec(memory_space=pltpu.SMEM), blk, cache, cache],
        out_specs=blk,
        out_shape=jax.ShapeDtypeStruct((b, s, ATTN_WIDTH), BF16),
        scratch_shapes=[pltpu.VMEM((SCRATCH_SETS, N_KV_HEADS, s, Q_PER_KV * s), F32),
                        pltpu.VMEM((SCRATCH_SETS, N_KV_HEADS, s, Q_PER_KV * s), BF16)],
        compiler_params=_params(1),
        name="context_attention",
    )(sink, q, k, v)


def _window_bias_t():
    c = np.arange(KEY_SPAN)[None, :, None]
    r = np.arange(Q_BLOCK)[None, None, :]
    p = np.arange(KEY_SPAN // Q_BLOCK)[:, None, None]
    return jnp.asarray(np.where(np.abs(c - (r + p * Q_BLOCK)) <= WINDOW, 0.0, NEG_INF).astype(np.float32))


def _attn_lat_kernel(sink_ref, q_ref, k_ref, vt_ref, kc_ref, vc_ref, bias_ref, o_ref, kcb_ref, vcb_ref, s_ref, p_ref,
                     *, layer):
    t = Q_BLOCK
    s_len = k_ref.shape[0]

    @pl.when(pl.program_id(1) == 0)
    def _():
        kcb_ref[...] = kc_ref[...].astype(BF16)
        vcb_ref[...] = _augment_values_t(vc_ref[...].T)

    context = (lambda kv: kcb_ref[:, _head_cols(kv)], lambda kv: vcb_ref[kv * AUG_ROWS:(kv + 1) * AUG_ROWS, :])
    for j in range(q_ref.shape[0] // t):
        i = pl.program_id(1) * (q_ref.shape[0] // t) + j
        start = pl.multiple_of(jnp.clip((i - 1) * t, 0, s_len - KEY_SPAN), t)
        pattern = lax.div(i * t - start, t)
        window = (lambda kv, start=start: k_ref[pl.ds(start, KEY_SPAN), _head_cols(kv)],
                  lambda kv, start=start: vt_ref[kv * AUG_ROWS:(kv + 1) * AUG_ROWS, pl.ds(start, KEY_SPAN)])
        qrows = slice(j * t, (j + 1) * t)
        _attend(lambda h, qrows=qrows: q_ref[qrows, _head_cols(h)], [window, context],
                lambda rows, pattern=pattern: bias_ref[pattern, rows, :], KEY_SPAN,
                lambda h: sink_ref[layer, h], s_ref.at[j % SCRATCH_SETS], p_ref.at[j % SCRATCH_SETS], o_ref.at[qrows])


def _latent_attention(q, k, v_t, cache_k, cache_v, sink, layer):
    b, s, _ = q.shape
    past = cache_k.shape[2]
    nq = min(LATENT_QBLOCKS_PER_STEP, s // Q_BLOCK)
    assert s % (nq * Q_BLOCK) == 0 and s >= KEY_SPAN and WINDOW <= Q_BLOCK and Q_BLOCK == LANES
    assert past % MAX_ROWS == 0
    n_keys = KEY_SPAN + past
    qblk = pl.BlockSpec((None, nq * Q_BLOCK, ATTN_WIDTH), lambda bi, i: (bi, i, 0))
    cache = pl.BlockSpec((None, None, past, KV_WIDTH), lambda bi, i: (bi, layer, 0, 0))
    bias = _window_bias_t()
    return pl.pallas_call(
        functools.partial(_attn_lat_kernel, layer=layer),
        grid=(b, s // (nq * Q_BLOCK)),
        in_specs=[pl.BlockSpec(memory_space=pltpu.SMEM), qblk,
                  pl.BlockSpec((None, s, KV_WIDTH), lambda bi, i: (bi, 0, 0)),
                  pl.BlockSpec((N_KV_HEADS * AUG_ROWS, s), lambda bi, i: (0, bi)),
                  cache, cache, _resident(bias.shape)],
        out_specs=qblk,
        out_shape=jax.ShapeDtypeStruct((b, s, ATTN_WIDTH), BF16),
        scratch_shapes=[pltpu.VMEM((past, KV_WIDTH), BF16), pltpu.VMEM((N_KV_HEADS * AUG_ROWS, past), BF16),
                        pltpu.VMEM((SCRATCH_SETS, N_KV_HEADS, n_keys, Q_PER_KV * Q_BLOCK), F32),
                        pltpu.VMEM((SCRATCH_SETS, N_KV_HEADS, n_keys, Q_PER_KV * Q_BLOCK), BF16)],
        compiler_params=_params(2),
        name="latent_attention",
    )(sink, q, k, v_t, cache_k, cache_v, bias)


def _dft_tables(s):
    def cos_sin(n, rows):
        k = np.arange(rows, dtype=np.int64)[:, None]
        j = np.arange(n, dtype=np.int64)[None, :]
        ang = 2.0 * np.pi * ((k * j) % n) / n
        scale = n ** -0.5
        return (np.cos(ang) * scale).astype(np.float32), (np.sin(ang) * scale).astype(np.float32)
    as_bf16 = lambda t: jnp.asarray(t).astype(BF16)
    cc, sc = cos_sin(FOURIER_GROUP_DIM, FOURIER_GROUP_DIM)
    chan = as_bf16(np.concatenate([cc, sc], axis=1))
    if s < SYMMETRIC_MIN_SEQ:
        cs, ss = cos_sin(s, s)
        return chan, as_bf16(cs), as_bf16(ss)
    half = s // 2
    cs, ss = cos_sin(s, half + TABLE_PAD)
    rev = np.zeros((half, half), np.float32)
    rev[np.arange(1, half), half - np.arange(1, half)] = 1.0
    return chan, as_bf16(cs), as_bf16(ss[:half]), as_bf16(rev)


def _fourier_kernel(x_ref, chan_ref, cos_ref, sin_ref, *rest, row_chunk):
    *rev_ref, o_ref, ab_ref, eo_ref = rest
    nb, s, _ = x_ref.shape
    n_direct = sin_ref.shape[0]
    gd = FOURIER_GROUP_DIM
    for b in range(nb):
        for r in range(0, s, row_chunk):
            for g in range(N_FOURIER_GROUPS):
                ab = _dot(x_ref[b, r:r + row_chunk, g * gd:(g + 1) * gd], chan_ref[...])
                ab_ref[r:r + row_chunk, g * gd:(g + 1) * gd] = ab[:, :gd].astype(BF16)
                ab_ref[s + r:s + r + row_chunk, g * gd:(g + 1) * gd] = ab[:, gd:].astype(BF16)
        for r in range(0, n_direct, row_chunk):
            rows = slice(r, r + row_chunk)
            e = _dot(cos_ref[rows, :], ab_ref[:s, :])
            o = _dot(sin_ref[rows, :], ab_ref[s:, :])
            o_ref[b, rows, :] = (e - o).astype(o_ref.dtype)
            if rev_ref:
                eo_ref[rows, :] = (e + o).astype(BF16)
        if rev_ref:
            e_mid = _dot(cos_ref[n_direct:n_direct + TABLE_PAD, :], ab_ref[:s, :])
            first = lax.broadcasted_iota(jnp.int32, e_mid.shape, 0) == 0
            for r in range(0, n_direct, row_chunk):
                up = _dot(rev_ref[0][r:r + row_chunk, :], eo_ref[...])
                if r == 0:
                    up = jnp.concatenate([up[:TABLE_PAD] + jnp.where(first, e_mid, 0.0), up[TABLE_PAD:]], axis=0)
                o_ref[b, n_direct + r:n_direct + r + row_chunk, :] = up.astype(o_ref.dtype)


def _fourier_mix(u, tables, seqs_per_step):
    b, s, w = u.shape
    n_direct = tables[2].shape[0]
    row_chunk = min(n_direct, 512)
    blk = pl.BlockSpec((seqs_per_step, s, w), lambda i: (i, 0, 0))
    return pl.pallas_call(
        functools.partial(_fourier_kernel, row_chunk=row_chunk),
        grid=(b // seqs_per_step,),
        in_specs=[blk] + [_resident(t.shape) for t in tables],
        out_specs=blk,
        out_shape=jax.ShapeDtypeStruct((b, s, w), BF16),
        scratch_shapes=[pltpu.VMEM((2 * s, w), BF16), pltpu.VMEM((n_direct, w), BF16)],
        compiler_params=_params(1),
        name="fourier_mix",
    )(u, *tables)


def _merge_mlp_kernel(*refs, final):
    (x_ref, f_ref, a_ref, gate_ref, ga1_ref, sh2_ref, sc2_ref, ga2_ref, g2_ref,
     wfo_ref, wao_ref, wout_ref, w1_ref, w2_ref) = refs[:14]
    o_ref = refs[-1]
    gate_f = gate_ref[:, :D_MODEL].astype(F32)
    gate_a = gate_ref[:, D_MODEL:].astype(F32)
    m = gate_f * _dot(f_ref[...], wfo_ref[...]) + gate_a * _dot(a_ref[...], wao_ref[...])
    x1 = x_ref[...] + ga1_ref[...] * _dot(m.astype(BF16), wout_ref[...])
    h = (_rmsnorm(x1, g2_ref[...]) * (1.0 + sc2_ref[...]) + sh2_ref[...]).astype(BF16)
    ff = None
    for j in range(D_FF // COL_CHUNK):
        cs = slice(j * COL_CHUNK, (j + 1) * COL_CHUNK)
        t = jnp.maximum(_dot(h, w1_ref[:, cs]), 0.0)
        part = _dot((t * t).astype(BF16), w2_ref[cs, :])
        ff = part if ff is None else ff + part
    x2 = x1 + ga2_ref[...] * ff
    if final:
        x2 = _rmsnorm(x2, refs[14][...])
    o_ref[...] = x2


def _merge_mlp(x, f, attn, gates, mod, layer, row_of_tile, g2, w_fo, w_ao, w_out, w_ff1, w_ff2, final_g, tm):
    rows = x.shape[0]
    tile = lambda w: pl.BlockSpec((tm, w), lambda i: (i, 0))
    in_specs = [tile(D_MODEL), tile(FOURIER_WIDTH), tile(ATTN_WIDTH), tile(2 * D_MODEL)]
    in_specs += [_mod_spec(layer, slot, row_of_tile) for slot in (2, 3, 4, 5)]
    in_specs += [_resident((1, D_MODEL), layer)]
    in_specs += [_resident(w.shape[1:], layer) for w in (w_fo, w_ao, w_out, w_ff1, w_ff2)]
    args = [x, f, attn, gates, mod, mod, mod, mod, g2, w_fo, w_ao, w_out, w_ff1, w_ff2]
    if final_g is not None:
        in_specs.append(_resident((1, D_MODEL)))
        args.append(final_g)
    return pl.pallas_call(
        functools.partial(_merge_mlp_kernel, final=final_g is not None),
        grid=(rows // tm,),
        in_specs=in_specs,
        out_specs=tile(D_MODEL),
        out_shape=jax.ShapeDtypeStruct((rows, D_MODEL), F32),
        compiler_params=_params(1),
        name="merge_mlp",
    )(*args)


def _rope_tables(n_tokens):
    t = jnp.arange(n_tokens)
    row = (t // GRID_W).astype(F32)
    col = (t % GRID_W).astype(F32)
    inv = ROPE_THETA ** (-jnp.arange(ROPE_PAIRS_PER_AXIS, dtype=F32) / ROPE_PAIRS_PER_AXIS)
    ang = jnp.concatenate([row[:, None] * inv[None, :], col[:, None] * inv[None, :]], axis=-1)
    cos = jnp.repeat(jnp.cos(ang), 2, axis=1)
    sin = jnp.repeat(jnp.sin(ang), 2, axis=1)
    even = (jnp.arange(HEAD_DIM) % 2 == 0)[None, :]
    reps = LANES // HEAD_DIM
    return (jnp.tile(cos, (1, reps)),
            jnp.tile(jnp.where(even, -sin, 0.0), (1, reps)),
            jnp.tile(jnp.where(even, 0.0, sin), (1, reps)))


def kernel(x_prompt, x_sample, c, cache_k, cache_v, c_ctx, w_ada, b_ada, norm1_g, norm2_g,
           w_in, sink, w_fo, w_ao, w_out, w_ff1, w_ff2, final_g):
    depth = w_in.shape[0]
    b_ctx, s_ctx, _ = x_prompt.shape
    b_lat, s_lat, _ = x_sample.shape
    past = cache_k.shape[2]
    assert 1 + b_lat <= MOD_ROWS
    tm, tm_in = ROW_TILE, INPROJ_ROW_TILE
    assert (b_ctx * s_ctx) % tm == 0 and s_lat % tm == 0 and (b_ctx * s_ctx) % tm_in == 0 and s_lat % tm_in == 0
    assert tm_in % SUB_TILE == 0 and SUB_TILE % s_ctx == 0

    cvecs = jnp.zeros((MOD_ROWS, D_MODEL), F32).at[0].set(c_ctx).at[1:1 + b_lat].set(c)
    mod = _modulation(cvecs, w_ada, b_ada).reshape(depth, MOD_ROWS, 6, 1, D_MODEL)
    ctx_row = lambda tile: lambda i: 0
    lat_row = lambda tile: lambda i: 1 + i // (s_lat // tile)

    w_in_b, w_fo_b, w_ao_b, w_out_b, w_ff1_b, w_ff2_b = (
        w.astype(BF16) for w in (w_in, w_fo, w_ao, w_out, w_ff1, w_ff2))
    w_v_t = jnp.swapaxes(w_in_b[:, :, OFF_V:OFF_G], 1, 2)
    g1 = norm1_g.reshape(depth, 1, D_MODEL)
    g2 = norm2_g.reshape(depth, 1, D_MODEL)
    gf = final_g.reshape(1, D_MODEL)
    rope_tabs = _rope_tables(s_lat)
    dft_ctx = _dft_tables(s_ctx)
    dft_lat = _dft_tables(s_lat)
    cache_k = cache_k.reshape(b_lat, depth, past, KV_WIDTH)
    cache_v = cache_v.reshape(b_lat, depth, past, KV_WIDTH)

    def layer_tail(x, uf, attn, gates, l, row_of_tile, fourier_tabs, batch, seqs_per_step):
        seq = x.shape[0] // batch
        f = _fourier_mix(uf.reshape(batch, seq, FOURIER_WIDTH), fourier_tabs, seqs_per_step)
        return _merge_mlp(x, f.reshape(-1, FOURIER_WIDTH), attn.reshape(-1, ATTN_WIDTH), gates, mod, l, row_of_tile,
                          g2, w_fo_b, w_ao_b, w_out_b, w_ff1_b, w_ff2_b, gf if l == depth - 1 else None, tm)

    xc = x_prompt.reshape(b_ctx * s_ctx, D_MODEL)
    kv_cache = (jnp.zeros((b_ctx, depth, s_ctx, KV_WIDTH), F32),) * 2
    for l in range(depth):
        uf, q, *kv_cache, gates = _in_projection(xc, mod, l, ctx_row(tm_in), g1, w_in_b, tm_in, kv_cache=kv_cache)
        attn = _context_attention(q.reshape(b_ctx, s_ctx, ATTN_WIDTH), *kv_cache, sink, l)
        xc = layer_tail(xc, uf, attn, gates, l, ctx_row(tm), dft_ctx, b_ctx, 4 if b_ctx % 4 == 0 else 1)
    y_prompt = xc.reshape(b_ctx, s_ctx, D_MODEL)
    new_k, new_v = (buf.reshape(b_ctx, depth, s_ctx, N_KV_HEADS, HEAD_DIM) for buf in kv_cache)

    xs = x_sample.reshape(b_lat * s_lat, D_MODEL)
    for l in range(depth):
        uf, q, k, v_t, gates = _in_projection(xs, mod, l, lat_row(tm_in), g1, w_in_b, tm_in, w_v_t=w_v_t,
                                              rope_tabs=rope_tabs)
        attn = _latent_attention(q.reshape(b_lat, s_lat, ATTN_WIDTH), k.reshape(b_lat, s_lat, KV_WIDTH),
                                 v_t, cache_k, cache_v, sink, l)
        xs = layer_tail(xs, uf, attn, gates, l, lat_row(tm), dft_lat, b_lat, 1)
    y_sample = xs.reshape(b_lat, s_lat, D_MODEL)

    return y_prompt, y_sample, new_k, new_v
```

```python
import functools

import numpy as np
import jax
import jax.numpy as jnp
from jax import lax
from jax.experimental import pallas as pl
from jax.experimental.pallas import tpu as pltpu

D_MODEL = 1024
N_HEADS = 16
N_KV_HEADS = 4
HEAD_DIM = 64
Q_PER_KV = N_HEADS // N_KV_HEADS
ATTN_WIDTH = N_HEADS * HEAD_DIM
KV_WIDTH = N_KV_HEADS * HEAD_DIM
N_FOURIER_GROUPS = 4
FOURIER_GROUP_DIM = 128
FOURIER_WIDTH = N_FOURIER_GROUPS * FOURIER_GROUP_DIM
IN_WIDTH = FOURIER_WIDTH + ATTN_WIDTH + 2 * KV_WIDTH + 2 * D_MODEL
D_FF = 4 * D_MODEL
GRID_W = 64
WINDOW = 128
ROPE_THETA = 10000.0
ROPE_PAIRS_PER_AXIS = HEAD_DIM // 4
EPS = 1e-6
NEG_INF = -1e30
LOG2E = 1.4426950408889634
QUERY_SCALE = HEAD_DIM ** -0.5 * LOG2E

OFF_Q = FOURIER_WIDTH
OFF_K = OFF_Q + ATTN_WIDTH
OFF_V = OFF_K + KV_WIDTH
OFF_G = OFF_V + KV_WIDTH

LANES = 128
MOD_ROWS = 16
VMEM_LIMIT_BYTES = 56 * 1024 * 1024
ROW_TILE = 512
INPROJ_ROW_TILE = 1024
SUB_TILE = 512
COL_CHUNK = 512
Q_BLOCK = 128
KEY_SPAN = 3 * Q_BLOCK
MAX_ROWS = 128
EXP_ROWS = 64
LATENT_QBLOCKS_PER_STEP = 8
CTX_SEQS_PER_STEP = 8
SCRATCH_SETS = 2
SYMMETRIC_MIN_SEQ = 1024
TABLE_PAD = 16
ONES_ROWS = 16
AUG_ROWS = HEAD_DIM + ONES_ROWS

BF16 = jnp.bfloat16
F32 = jnp.float32


def _params(n_axes):
    return pltpu.CompilerParams(dimension_semantics=("arbitrary",) * n_axes,
                                vmem_limit_bytes=VMEM_LIMIT_BYTES)


def _resident(shape, layer=None):
    zeros = (0,) * len(shape)
    if layer is None:
        return pl.BlockSpec(shape, lambda *_: zeros, pipeline_mode=pl.Buffered(1))
    return pl.BlockSpec((None,) + tuple(shape), lambda *_: (layer,) + zeros, pipeline_mode=pl.Buffered(1))


def _dot(a, b):
    return jnp.dot(a, b, preferred_element_type=F32)


def _sigmoid(z):
    return 1.0 / (1.0 + jnp.exp(-z))


def _rmsnorm(x, g):
    return x * lax.rsqrt(jnp.mean(x * x, axis=-1, keepdims=True) + EPS) * g


def _mod_kernel(c_ref, w_ref, b_ref, o_ref):
    c = c_ref[...]
    a = c * _sigmoid(c)
    a_hi = a.astype(BF16)
    a_lo = (a - a_hi.astype(F32)).astype(BF16)
    w = w_ref[...]
    w_hi = w.astype(BF16)
    w_lo = (w - w_hi.astype(F32)).astype(BF16)
    o_ref[...] = _dot(a_hi, w_hi) + _dot(a_lo, w_hi) + _dot(a_hi, w_lo) + b_ref[...]


def _modulation(cvecs, w_ada, b_ada):
    depth, _, width = w_ada.shape
    tn = width // 4
    return pl.pallas_call(
        _mod_kernel,
        grid=(depth, width // tn),
        in_specs=[
            pl.BlockSpec((MOD_ROWS, D_MODEL), lambda l, j: (0, 0)),
            pl.BlockSpec((None, D_MODEL, tn), lambda l, j: (l, 0, j)),
            pl.BlockSpec((None, 1, tn), lambda l, j: (l, 0, j)),
        ],
        out_specs=pl.BlockSpec((None, MOD_ROWS, tn), lambda l, j: (l, 0, j)),
        out_shape=jax.ShapeDtypeStruct((depth, MOD_ROWS, width), F32),
        compiler_params=_params(2),
        name="modulation",
    )(cvecs, w_ada, b_ada.reshape(depth, 1, width))


def _mod_spec(layer, slot, row_of_tile):
    return pl.BlockSpec((None, None, None, 1, D_MODEL),
                        lambda i: (layer, row_of_tile(i), slot, 0, 0))


def _rope(z, cos, sin_even, sin_odd):
    outs = []
    for j in range(z.shape[1] // LANES):
        zj = z[:, j * LANES:(j + 1) * LANES]
        outs.append(zj * cos + pltpu.roll(zj, LANES - 1, 1) * sin_even + pltpu.roll(zj, 1, 1) * sin_odd)
    return outs[0] if len(outs) == 1 else jnp.concatenate(outs, axis=1)


def _augment_values_t(v_t):
    ones = jnp.ones((ONES_ROWS, v_t.shape[1]), v_t.dtype)
    parts = []
    for kv in range(N_KV_HEADS):
        parts += [v_t[kv * HEAD_DIM:(kv + 1) * HEAD_DIM], ones]
    return jnp.concatenate(parts, axis=0).astype(BF16)


def _inproj_kernel(*refs, rope):
    if rope:
        (x_ref, sh_ref, sc_ref, g_ref, w_ref, wvt_ref, cos_ref, se_ref, so_ref,
         uf_ref, q_ref, k_ref, v_ref, gate_ref) = refs
        rot = lambda z, rows: _rope(z, cos_ref[rows, :], se_ref[rows, :], so_ref[rows, :])
    else:
        x_ref, sh_ref, sc_ref, g_ref, w_ref, _, _, uf_ref, q_ref, k_ref, v_ref, gate_ref = refs
        rot = lambda z, rows: z
    for r0 in range(0, x_ref.shape[0], SUB_TILE):
        rows = slice(r0, r0 + SUB_TILE)
        h = _rmsnorm(x_ref[rows, :], g_ref[...]) * (1.0 + sc_ref[...]) + sh_ref[...]
        h = h.astype(BF16)

        def seg(start, h=h):
            return _dot(h, w_ref[:, start:start + COL_CHUNK])

        uf_ref[rows, :] = seg(0).astype(uf_ref.dtype)
        for j in range(ATTN_WIDTH // COL_CHUNK):
            z = rot(seg(OFF_Q + j * COL_CHUNK), rows) * QUERY_SCALE
            q_ref[rows, j * COL_CHUNK:(j + 1) * COL_CHUNK] = z.astype(q_ref.dtype)
        if rope:
            k_ref[rows, :] = rot(_dot(h, w_ref[:, OFF_K:OFF_V]), rows).astype(k_ref.dtype)
            v_t = lax.dot_general(wvt_ref[...], h, (((1,), (1,)), ((), ())), preferred_element_type=F32)
            v_ref[:, rows] = _augment_values_t(v_t)
        else:
            z = seg(OFF_K)
            seqs = slice(r0 // k_ref.shape[1], (r0 + SUB_TILE) // k_ref.shape[1])
            k_ref[seqs] = z[:, :KV_WIDTH].reshape(k_ref[seqs].shape)
            v_ref[seqs] = z[:, KV_WIDTH:].reshape(v_ref[seqs].shape)
        for j in range(2 * D_MODEL // COL_CHUNK):
            z = seg(OFF_G + j * COL_CHUNK)
            gate_ref[rows, j * COL_CHUNK:(j + 1) * COL_CHUNK] = _sigmoid(z).astype(gate_ref.dtype)


def _in_projection(x, mod, layer, row_of_tile, g1, w_in, tm, *, w_v_t=None, rope_tabs=None, kv_cache=None):
    rows = x.shape[0]
    tile = lambda w: pl.BlockSpec((tm, w), lambda i: (i, 0))
    in_specs = [tile(D_MODEL), _mod_spec(layer, 0, row_of_tile), _mod_spec(layer, 1, row_of_tile),
                _resident((1, D_MODEL), layer), _resident((D_MODEL, IN_WIDTH), layer)]
    args = [x, mod, mod, g1, w_in]
    out_specs = [tile(FOURIER_WIDTH), tile(ATTN_WIDTH), None, None, tile(2 * D_MODEL)]
    out_shape = [jax.ShapeDtypeStruct((rows, FOURIER_WIDTH), BF16), jax.ShapeDtypeStruct((rows, ATTN_WIDTH), BF16),
                 None, None, jax.ShapeDtypeStruct((rows, 2 * D_MODEL), BF16)]
    latent = rope_tabs is not None
    aliases = {}
    if latent:
        tiles_per_seq = rope_tabs[0].shape[0] // tm
        in_specs += [_resident((KV_WIDTH, D_MODEL), layer)]
        in_specs += [pl.BlockSpec((tm, LANES), lambda i: (i % tiles_per_seq, 0))] * 3
        args += [w_v_t] + list(rope_tabs)
        out_specs[2] = tile(KV_WIDTH)
        out_shape[2] = jax.ShapeDtypeStruct((rows, KV_WIDTH), BF16)
        out_specs[3] = pl.BlockSpec((N_KV_HEADS * AUG_ROWS, tm), lambda i: (0, i))
        out_shape[3] = jax.ShapeDtypeStruct((N_KV_HEADS * AUG_ROWS, rows), BF16)
    else:
        s_ctx = kv_cache[0].shape[2]
        assert tm % s_ctx == 0
        in_specs += [pl.BlockSpec(memory_space=pl.ANY)] * 2
        args += list(kv_cache)
        aliases = {5: 2, 6: 3}
        for slot in (2, 3):
            out_specs[slot] = pl.BlockSpec((tm // s_ctx, None, s_ctx, KV_WIDTH), lambda i: (i, layer, 0, 0))
            out_shape[slot] = jax.ShapeDtypeStruct(kv_cache[0].shape, F32)
    return pl.pallas_call(
        functools.partial(_inproj_kernel, rope=latent),
        grid=(rows // tm,),
        in_specs=in_specs,
        out_specs=out_specs,
        out_shape=out_shape,
        input_output_aliases=aliases,
        compiler_params=_params(1),
        name="in_projection",
    )(*args)


def _softmax_head(s_ref, p_ref, cols, sink):
    n_keys = s_ref.shape[0]
    blocks = [s_ref[r:r + MAX_ROWS, cols] for r in range(0, n_keys, MAX_ROWS)]
    m = jnp.maximum(jnp.max(functools.reduce(jnp.maximum, blocks), axis=0, keepdims=True), sink)
    for r in range(0, n_keys, EXP_ROWS):
        rows = slice(r, r + EXP_ROWS)
        p_ref[rows, cols] = jnp.exp2((s_ref[rows, cols] - m).astype(BF16))
    return jnp.exp2(sink - m)


def _attend(q_of, key_parts, bias_t_of, n_biased, sink_of, s_ref, p_ref, o_ref):
    t = o_ref.shape[0]

    def score(kv):
        qs = jnp.concatenate([q_of(kv * Q_PER_KV + g) for g in range(Q_PER_KV)], axis=0)
        row = 0
        for k_of, _ in key_parts:
            k = k_of(kv)
            part = lax.dot_general(k, qs, (((1,), (1,)), ((), ())), preferred_element_type=F32)
            if row < n_biased:
                part = part + jnp.concatenate([bias_t_of(slice(row, row + k.shape[0]))] * Q_PER_KV, axis=1)
            s_ref[kv, row:row + k.shape[0], :] = part
            row += k.shape[0]

    score(0)
    for kv in range(N_KV_HEADS):
        if kv + 1 < N_KV_HEADS:
            score(kv + 1)
        sink_terms = []
        for g in range(Q_PER_KV):
            sink = sink_of(kv * Q_PER_KV + g) * LOG2E
            sink_terms.append(_softmax_head(s_ref.at[kv], p_ref.at[kv], slice(g * t, (g + 1) * t), sink))
        o_aug, row = None, 0
        for _, v_t_of in key_parts:
            v_t = v_t_of(kv)
            part = _dot(v_t, p_ref[kv, row:row + v_t.shape[1], :])
            o_aug = part if o_aug is None else o_aug + part
            row += v_t.shape[1]
        denom = o_aug[HEAD_DIM:HEAD_DIM + 1, :] + jnp.concatenate(sink_terms, axis=1)
        o_t = o_aug[:HEAD_DIM, :] / denom
        for pair in range(Q_PER_KV // 2):
            two = jnp.concatenate([o_t[:, (2 * pair) * t:(2 * pair + 1) * t],
                                   o_t[:, (2 * pair + 1) * t:(2 * pair + 2) * t]], axis=0)
            col = (kv * (Q_PER_KV // 2) + pair) * LANES
            o_ref[:, col:col + LANES] = two.T.astype(o_ref.dtype)


def _head_cols(head):
    return slice(head * HEAD_DIM, (head + 1) * HEAD_DIM)


def _attn_ctx_kernel(sink_ref, q_ref, k_ref, v_ref, o_ref, s_ref, p_ref, *, layer):
    for b in range(q_ref.shape[0]):
        k = k_ref[b].astype(BF16)
        v_t = _augment_values_t(v_ref[b].T)
        _attend(lambda h: q_ref[b, :, _head_cols(h)],
                [(lambda kv: k[:, _head_cols(kv)], lambda kv: v_t[kv * AUG_ROWS:(kv + 1) * AUG_ROWS])],
                None, 0, lambda h: sink_ref[layer, h], s_ref.at[b % SCRATCH_SETS], p_ref.at[b % SCRATCH_SETS], o_ref.at[b])


def _context_attention(q, k, v, sink, layer):
    b, s, _ = q.shape
    nb = CTX_SEQS_PER_STEP
    assert s % MAX_ROWS == 0 and b % nb == 0
    blk = pl.BlockSpec((nb, s, ATTN_WIDTH), lambda i: (i, 0, 0))
    cache = pl.BlockSpec((nb, None, s, KV_WIDTH), lambda i: (i, layer, 0, 0))
    return pl.pallas_call(
        functools.partial(_attn_ctx_kernel, layer=layer),
        grid=(b // nb,),
        in_specs=[pl.BlockSpec(memory_space=pltpu.SMEM), blk, cache, cache],
        out_specs=blk,
        out_shape=jax.ShapeDtypeStruct((b, s, ATTN_WIDTH), BF16),
        scratch_shapes=[pltpu.VMEM((SCRATCH_SETS, N_KV_HEADS, s, Q_PER_KV * s), F32),
                        pltpu.VMEM((SCRATCH_SETS, N_KV_HEADS, s, Q_PER_KV * s), BF16)],
        compiler_params=_params(1),
        name="context_attention",
    )(sink, q, k, v)


def _window_bias_t():
    c = np.arange(KEY_SPAN)[None, :, None]
    r = np.arange(Q_BLOCK)[None, None, :]
    p = np.arange(KEY_SPAN // Q_BLOCK)[:, None, None]
    return jnp.asarray(np.where(np.abs(c - (r + p * Q_BLOCK)) <= WINDOW, 0.0, NEG_INF).astype(np.float32))


def _attn_lat_kernel(sink_ref, q_ref, k_ref, vt_ref, kc_ref, vc_ref, bias_ref, o_ref, kcb_ref, vcb_ref, s_ref, p_ref,
                     *, layer):
    t = Q_BLOCK
    s_len = k_ref.shape[0]

    @pl.when(pl.program_id(1) == 0)
    def _():
        kcb_ref[...] = kc_ref[...].astype(BF16)
        vcb_ref[...] = _augment_values_t(vc_ref[...].T)

    context = (lambda kv: kcb_ref[:, _head_cols(kv)], lambda kv: vcb_ref[kv * AUG_ROWS:(kv + 1) * AUG_ROWS, :])
    for j in range(q_ref.shape[0] // t):
        i = pl.program_id(1) * (q_ref.shape[0] // t) + j
        start = pl.multiple_of(jnp.clip((i - 1) * t, 0, s_len - KEY_SPAN), t)
        pattern = lax.div(i * t - start, t)
        window = (lambda kv, start=start: k_ref[pl.ds(start, KEY_SPAN), _head_cols(kv)],
                  lambda kv, start=start: vt_ref[kv * AUG_ROWS:(kv + 1) * AUG_ROWS, pl.ds(start, KEY_SPAN)])
        qrows = slice(j * t, (j + 1) * t)
        _attend(lambda h, qrows=qrows: q_ref[qrows, _head_cols(h)], [window, context],
                lambda rows, pattern=pattern: bias_ref[pattern, rows, :], KEY_SPAN,
                lambda h: sink_ref[layer, h], s_ref.at[j % SCRATCH_SETS], p_ref.at[j % SCRATCH_SETS], o_ref.at[qrows])


def _latent_attention(q, k, v_t, cache_k, cache_v, sink, layer):
    b, s, _ = q.shape
    past = cache_k.shape[2]
    nq = LATENT_QBLOCKS_PER_STEP
    assert s % (nq * Q_BLOCK) == 0 and s >= KEY_SPAN and WINDOW <= Q_BLOCK and Q_BLOCK == LANES
    assert past % MAX_ROWS == 0
    n_keys = KEY_SPAN + past
    qblk = pl.BlockSpec((None, nq * Q_BLOCK, ATTN_WIDTH), lambda bi, i: (bi, i, 0))
    cache = pl.BlockSpec((None, None, past, KV_WIDTH), lambda bi, i: (bi, layer, 0, 0))
    bias = _window_bias_t()
    return pl.pallas_call(
        functools.partial(_attn_lat_kernel, layer=layer),
        grid=(b, s // (nq * Q_BLOCK)),
        in_specs=[pl.BlockSpec(memory_space=pltpu.SMEM), qblk,
                  pl.BlockSpec((None, s, KV_WIDTH), lambda bi, i: (bi, 0, 0)),
                  pl.BlockSpec((N_KV_HEADS * AUG_ROWS, s), lambda bi, i: (0, bi)),
                  cache, cache, _resident(bias.shape)],
        out_specs=qblk,
        out_shape=jax.ShapeDtypeStruct((b, s, ATTN_WIDTH), BF16),
        scratch_shapes=[pltpu.VMEM((past, KV_WIDTH), BF16), pltpu.VMEM((N_KV_HEADS * AUG_ROWS, past), BF16),
                        pltpu.VMEM((SCRATCH_SETS, N_KV_HEADS, n_keys, Q_PER_KV * Q_BLOCK), F32),
                        pltpu.VMEM((SCRATCH_SETS, N_KV_HEADS, n_keys, Q_PER_KV * Q_BLOCK), BF16)],
        compiler_params=_params(2),
        name="latent_attention",
    )(sink, q, k, v_t, cache_k, cache_v, bias)


def _dft_tables(s):
    def cos_sin(n, rows):
        k = np.arange(rows, dtype=np.int64)[:, None]
        j = np.arange(n, dtype=np.int64)[None, :]
        ang = 2.0 * np.pi * ((k * j) % n) / n
        scale = n ** -0.5
        return (np.cos(ang) * scale).astype(np.float32), (np.sin(ang) * scale).astype(np.float32)
    as_bf16 = lambda t: jnp.asarray(t).astype(BF16)
    cc, sc = cos_sin(FOURIER_GROUP_DIM, FOURIER_GROUP_DIM)
    chan = as_bf16(np.concatenate([cc, sc], axis=1))
    if s < SYMMETRIC_MIN_SEQ:
        cs, ss = cos_sin(s, s)
        return chan, as_bf16(cs), as_bf16(ss)
    half = s // 2
    cs, ss = cos_sin(s, half + TABLE_PAD)
    rev = np.zeros((half, half), np.float32)
    rev[np.arange(1, half), half - np.arange(1, half)] = 1.0
    return chan, as_bf16(cs), as_bf16(ss[:half]), as_bf16(rev)


def _fourier_kernel(x_ref, chan_ref, cos_ref, sin_ref, *rest, row_chunk):
    *rev_ref, o_ref, ab_ref, eo_ref = rest
    nb, s, _ = x_ref.shape
    n_direct = sin_ref.shape[0]
    gd = FOURIER_GROUP_DIM
    for b in range(nb):
        for r in range(0, s, row_chunk):
            for g in range(N_FOURIER_GROUPS):
                ab = _dot(x_ref[b, r:r + row_chunk, g * gd:(g + 1) * gd], chan_ref[...])
                ab_ref[r:r + row_chunk, g * gd:(g + 1) * gd] = ab[:, :gd].astype(BF16)
                ab_ref[s + r:s + r + row_chunk, g * gd:(g + 1) * gd] = ab[:, gd:].astype(BF16)
        for r in range(0, n_direct, row_chunk):
            rows = slice(r, r + row_chunk)
            e = _dot(cos_ref[rows, :], ab_ref[:s, :])
            o = _dot(sin_ref[rows, :], ab_ref[s:, :])
            o_ref[b, rows, :] = (e - o).astype(o_ref.dtype)
            if rev_ref:
                eo_ref[rows, :] = (e + o).astype(BF16)
        if rev_ref:
            e_mid = _dot(cos_ref[n_direct:n_direct + TABLE_PAD, :], ab_ref[:s, :])
            first = lax.broadcasted_iota(jnp.int32, e_mid.shape, 0) == 0
            for r in range(0, n_direct, row_chunk):
                up = _dot(rev_ref[0][r:r + row_chunk, :], eo_ref[...])
                if r == 0:
                    up = jnp.concatenate([up[:TABLE_PAD] + jnp.where(first, e_mid, 0.0), up[TABLE_PAD:]], axis=0)
                o_ref[b, n_direct + r:n_direct + r + row_chunk, :] = up.astype(o_ref.dtype)


def _fourier_mix(u, tables, seqs_per_step):
    b, s, w = u.shape
    n_direct = tables[2].shape[0]
    row_chunk = min(n_direct, 512)
    blk = pl.BlockSpec((seqs_per_step, s, w), lambda i: (i, 0, 0))
    return pl.pallas_call(
        functools.partial(_fourier_kernel, row_chunk=row_chunk),
        grid=(b // seqs_per_step,),
        in_specs=[blk] + [_resident(t.shape) for t in tables],
        out_specs=blk,
        out_shape=jax.ShapeDtypeStruct((b, s, w), BF16),
        scratch_shapes=[pltpu.VMEM((2 * s, w), BF16), pltpu.VMEM((n_direct, w), BF16)],
        compiler_params=_params(1),
        name="fourier_mix",
    )(u, *tables)


def _merge_mlp_kernel(*refs, final):
    (x_ref, f_ref, a_ref, gate_ref, ga1_ref, sh2_ref, sc2_ref, ga2_ref, g2_ref,
     wfo_ref, wao_ref, wout_ref, w1_ref, w2_ref) = refs[:14]
    o_ref = refs[-1]
    gate_f = gate_ref[:, :D_MODEL].astype(F32)
    gate_a = gate_ref[:, D_MODEL:].astype(F32)
    m = gate_f * _dot(f_ref[...], wfo_ref[...]) + gate_a * _dot(a_ref[...], wao_ref[...])
    x1 = x_ref[...] + ga1_ref[...] * _dot(m.astype(BF16), wout_ref[...])
    h = (_rmsnorm(x1, g2_ref[...]) * (1.0 + sc2_ref[...]) + sh2_ref[...]).astype(BF16)
    ff = None
    for j in range(D_FF // COL_CHUNK):
        cs = slice(j * COL_CHUNK, (j + 1) * COL_CHUNK)
        t = jnp.maximum(_dot(h, w1_ref[:, cs]), 0.0)
        part = _dot((t * t).astype(BF16), w2_ref[cs, :])
        ff = part if ff is None else ff + part
    x2 = x1 + ga2_ref[...] * ff
    if final:
        x2 = _rmsnorm(x2, refs[14][...])
    o_ref[...] = x2


def _merge_mlp(x, f, attn, gates, mod, layer, row_of_tile, g2, w_fo, w_ao, w_out, w_ff1, w_ff2, final_g, tm):
    rows = x.shape[0]
    tile = lambda w: pl.BlockSpec((tm, w), lambda i: (i, 0))
    in_specs = [tile(D_MODEL), tile(FOURIER_WIDTH), tile(ATTN_WIDTH), tile(2 * D_MODEL)]
    in_specs += [_mod_spec(layer, slot, row_of_tile) for slot in (2, 3, 4, 5)]
    in_specs += [_resident((1, D_MODEL), layer)]
    in_specs += [_resident(w.shape[1:], layer) for w in (w_fo, w_ao, w_out, w_ff1, w_ff2)]
    args = [x, f, attn, gates, mod, mod, mod, mod, g2, w_fo, w_ao, w_out, w_ff1, w_ff2]
    if final_g is not None:
        in_specs.append(_resident((1, D_MODEL)))
        args.append(final_g)
    return pl.pallas_call(
        functools.partial(_merge_mlp_kernel, final=final_g is not None),
        grid=(rows // tm,),
        in_specs=in_specs,
        out_specs=tile(D_MODEL),
        out_shape=jax.ShapeDtypeStruct((rows, D_MODEL), F32),
        compiler_params=_params(1),
        name="merge_mlp",
    )(*args)


def _rope_tables(n_tokens):
    t = jnp.arange(n_tokens)
    row = (t // GRID_W).astype(F32)
    col = (t % GRID_W).astype(F32)
    inv = ROPE_THETA ** (-jnp.arange(ROPE_PAIRS_PER_AXIS, dtype=F32) / ROPE_PAIRS_PER_AXIS)
    ang = jnp.concatenate([row[:, None] * inv[None, :], col[:, None] * inv[None, :]], axis=-1)
    cos = jnp.repeat(jnp.cos(ang), 2, axis=1)
    sin = jnp.repeat(jnp.sin(ang), 2, axis=1)
    even = (jnp.arange(HEAD_DIM) % 2 == 0)[None, :]
    reps = LANES // HEAD_DIM
    return (jnp.tile(cos, (1, reps)),
            jnp.tile(jnp.where(even, -sin, 0.0), (1, reps)),
            jnp.tile(jnp.where(even, 0.0, sin), (1, reps)))


def kernel(x_prompt, x_sample, c, cache_k, cache_v, c_ctx, w_ada, b_ada, norm1_g, norm2_g,
           w_in, sink, w_fo, w_ao, w_out, w_ff1, w_ff2, final_g):
    depth = w_in.shape[0]
    b_ctx, s_ctx, _ = x_prompt.shape
    b_lat, s_lat, _ = x_sample.shape
    past = cache_k.shape[2]
    assert 1 + b_lat <= MOD_ROWS
    tm, tm_in = ROW_TILE, INPROJ_ROW_TILE
    assert (b_ctx * s_ctx) % tm == 0 and s_lat % tm == 0 and (b_ctx * s_ctx) % tm_in == 0 and s_lat % tm_in == 0
    assert tm_in % SUB_TILE == 0 and SUB_TILE % s_ctx == 0

    cvecs = jnp.zeros((MOD_ROWS, D_MODEL), F32).at[0].set(c_ctx).at[1:1 + b_lat].set(c)
    mod = _modulation(cvecs, w_ada, b_ada).reshape(depth, MOD_ROWS, 6, 1, D_MODEL)
    ctx_row = lambda tile: lambda i: 0
    lat_row = lambda tile: lambda i: 1 + i // (s_lat // tile)

    w_in_b, w_fo_b, w_ao_b, w_out_b, w_ff1_b, w_ff2_b = (
        w.astype(BF16) for w in (w_in, w_fo, w_ao, w_out, w_ff1, w_ff2))
    w_v_t = jnp.swapaxes(w_in_b[:, :, OFF_V:OFF_G], 1, 2)
    g1 = norm1_g.reshape(depth, 1, D_MODEL)
    g2 = norm2_g.reshape(depth, 1, D_MODEL)
    gf = final_g.reshape(1, D_MODEL)
    rope_tabs = _rope_tables(s_lat)
    dft_ctx = _dft_tables(s_ctx)
    dft_lat = _dft_tables(s_lat)
    cache_k = cache_k.reshape(b_lat, depth, past, KV_WIDTH)
    cache_v = cache_v.reshape(b_lat, depth, past, KV_WIDTH)

    def layer_tail(x, uf, attn, gates, l, row_of_tile, fourier_tabs, batch, seqs_per_step):
        seq = x.shape[0] // batch
        f = _fourier_mix(uf.reshape(batch, seq, FOURIER_WIDTH), fourier_tabs, seqs_per_step)
        return _merge_mlp(x, f.reshape(-1, FOURIER_WIDTH), attn.reshape(-1, ATTN_WIDTH), gates, mod, l, row_of_tile,
                          g2, w_fo_b, w_ao_b, w_out_b, w_ff1_b, w_ff2_b, gf if l == depth - 1 else None, tm)

    xc = x_prompt.reshape(b_ctx * s_ctx, D_MODEL)
    kv_cache = (jnp.zeros((b_ctx, depth, s_ctx, KV_WIDTH), F32),) * 2
    for l in range(depth):
        uf, q, *kv_cache, gates = _in_projection(xc, mod, l, ctx_row(tm_in), g1, w_in_b, tm_in, kv_cache=kv_cache)
        attn = _context_attention(q.reshape(b_ctx, s_ctx, ATTN_WIDTH), *kv_cache, sink, l)
        xc = layer_tail(xc, uf, attn, gates, l, ctx_row(tm), dft_ctx, b_ctx, 4 if b_ctx % 4 == 0 else 1)
    y_prompt = xc.reshape(b_ctx, s_ctx, D_MODEL)
    new_k, new_v = (buf.reshape(b_ctx, depth, s_ctx, N_KV_HEADS, HEAD_DIM) for buf in kv_cache)

    xs = x_sample.reshape(b_lat * s_lat, D_MODEL)
    for l in range(depth):
        uf, q, k, v_t, gates = _in_projection(xs, mod, l, lat_row(tm_in), g1, w_in_b, tm_in, w_v_t=w_v_t,
                                              rope_tabs=rope_tabs)
        attn = _latent_attention(q.reshape(b_lat, s_lat, ATTN_WIDTH), k.reshape(b_lat, s_lat, KV_WIDTH),
                                 v_t, cache_k, cache_v, sink, l)
        xs = layer_tail(xs, uf, attn, gates, l, lat_row(tm), dft_lat, b_lat, 1)
    y_sample = xs.reshape(b_lat, s_lat, D_MODEL)

    return y_prompt, y_sample, new_k, new_v
```

```python
import functools

import numpy as np
import jax
import jax.numpy as jnp
from jax import lax
from jax.experimental import pallas as pl
from jax.experimental.pallas import tpu as pltpu

D_MODEL = 1024
N_HEADS = 16
N_KV_HEADS = 4
HEAD_DIM = 64
Q_PER_KV = N_HEADS // N_KV_HEADS
ATTN_WIDTH = N_HEADS * HEAD_DIM
KV_WIDTH = N_KV_HEADS * HEAD_DIM
N_FOURIER_GROUPS = 4
FOURIER_GROUP_DIM = 128
FOURIER_WIDTH = N_FOURIER_GROUPS * FOURIER_GROUP_DIM
IN_WIDTH = FOURIER_WIDTH + ATTN_WIDTH + 2 * KV_WIDTH + 2 * D_MODEL
D_FF = 4 * D_MODEL
GRID_W = 64
WINDOW = 128
ROPE_THETA = 10000.0
ROPE_PAIRS_PER_AXIS = HEAD_DIM // 4
EPS = 1e-6
NEG_INF = -1e30
LOG2E = 1.4426950408889634
QUERY_SCALE = HEAD_DIM ** -0.5 * LOG2E

OFF_Q = FOURIER_WIDTH
OFF_K = OFF_Q + ATTN_WIDTH
OFF_V = OFF_K + KV_WIDTH
OFF_G = OFF_V + KV_WIDTH

LANES = 128
MOD_ROWS = 16
VMEM_LIMIT_BYTES = 56 * 1024 * 1024
ROW_TILE = 512
INPROJ_ROW_TILE = 1024
SUB_TILE = 512
COL_CHUNK = 512
Q_BLOCK = 128
KEY_SPAN = 3 * Q_BLOCK
MAX_ROWS = 128
EXP_ROWS = 64
LATENT_QBLOCKS_PER_STEP = 8
CTX_SEQS_PER_STEP = 8
SCRATCH_SETS = 2
SYMMETRIC_MIN_SEQ = 1024
TABLE_PAD = 16
ONES_ROWS = 16
AUG_ROWS = HEAD_DIM + ONES_ROWS

BF16 = jnp.bfloat16
F32 = jnp.float32


def _params(n_axes):
    return pltpu.CompilerParams(dimension_semantics=("arbitrary",) * n_axes,
                                vmem_limit_bytes=VMEM_LIMIT_BYTES)


def _resident(shape, layer=None):
    zeros = (0,) * len(shape)
    if layer is None:
        return pl.BlockSpec(shape, lambda *_: zeros, pipeline_mode=pl.Buffered(1))
    return pl.BlockSpec((None,) + tuple(shape), lambda *_: (layer,) + zeros, pipeline_mode=pl.Buffered(1))


def _dot(a, b):
    return jnp.dot(a, b, preferred_element_type=F32)


def _sigmoid(z):
    return 1.0 / (1.0 + jnp.exp(-z))


def _rmsnorm(x, g):
    return x * lax.rsqrt(jnp.mean(x * x, axis=-1, keepdims=True) + EPS) * g


def _mod_kernel(c_ref, w_ref, b_ref, o_ref):
    c = c_ref[...]
    a = c * _sigmoid(c)
    a_hi = a.astype(BF16)
    a_lo = (a - a_hi.astype(F32)).astype(BF16)
    w = w_ref[...]
    w_hi = w.astype(BF16)
    w_lo = (w - w_hi.astype(F32)).astype(BF16)
    o_ref[...] = _dot(a_hi, w_hi) + _dot(a_lo, w_hi) + _dot(a_hi, w_lo) + b_ref[...]


def _modulation(cvecs, w_ada, b_ada):
    depth, _, width = w_ada.shape
    tn = width // 4
    return pl.pallas_call(
        _mod_kernel,
        grid=(depth, width // tn),
        in_specs=[
            pl.BlockSpec((MOD_ROWS, D_MODEL), lambda l, j: (0, 0)),
            pl.BlockSpec((None, D_MODEL, tn), lambda l, j: (l, 0, j)),
            pl.BlockSpec((None, 1, tn), lambda l, j: (l, 0, j)),
        ],
        out_specs=pl.BlockSpec((None, MOD_ROWS, tn), lambda l, j: (l, 0, j)),
        out_shape=jax.ShapeDtypeStruct((depth, MOD_ROWS, width), F32),
        compiler_params=_params(2),
        name="modulation",
    )(cvecs, w_ada, b_ada.reshape(depth, 1, width))


def _mod_spec(layer, slot, row_of_tile):
    return pl.BlockSpec((None, None, None, 1, D_MODEL),
                        lambda i: (layer, row_of_tile(i), slot, 0, 0))


def _rope(z, cos, sin_even, sin_odd):
    outs = []
    for j in range(z.shape[1] // LANES):
        zj = z[:, j * LANES:(j + 1) * LANES]
        outs.append(zj * cos + pltpu.roll(zj, LANES - 1, 1) * sin_even + pltpu.roll(zj, 1, 1) * sin_odd)
    return outs[0] if len(outs) == 1 else jnp.concatenate(outs, axis=1)


def _augment_values_t(v_t):
    ones = jnp.ones((ONES_ROWS, v_t.shape[1]), v_t.dtype)
    parts = []
    for kv in range(N_KV_HEADS):
        parts += [v_t[kv * HEAD_DIM:(kv + 1) * HEAD_DIM], ones]
    return jnp.concatenate(parts, axis=0).astype(BF16)


def _inproj_kernel(*refs, rope):
    if rope:
        (x_ref, sh_ref, sc_ref, g_ref, w_ref, wvt_ref, cos_ref, se_ref, so_ref,
         uf_ref, q_ref, k_ref, v_ref, gate_ref) = refs
        rot = lambda z, rows: _rope(z, cos_ref[rows, :], se_ref[rows, :], so_ref[rows, :])
    else:
        x_ref, sh_ref, sc_ref, g_ref, w_ref, _, _, uf_ref, q_ref, k_ref, v_ref, gate_ref = refs
        rot = lambda z, rows: z
    for r0 in range(0, x_ref.shape[0], SUB_TILE):
        rows = slice(r0, r0 + SUB_TILE)
        h = _rmsnorm(x_ref[rows, :], g_ref[...]) * (1.0 + sc_ref[...]) + sh_ref[...]
        h = h.astype(BF16)

        def seg(start, h=h):
            return _dot(h, w_ref[:, start:start + COL_CHUNK])

        uf_ref[rows, :] = seg(0).astype(uf_ref.dtype)
        for j in range(ATTN_WIDTH // COL_CHUNK):
            z = rot(seg(OFF_Q + j * COL_CHUNK), rows) * QUERY_SCALE
            q_ref[rows, j * COL_CHUNK:(j + 1) * COL_CHUNK] = z.astype(q_ref.dtype)
        if rope:
            k_ref[rows, :] = rot(_dot(h, w_ref[:, OFF_K:OFF_V]), rows).astype(k_ref.dtype)
            v_t = lax.dot_general(wvt_ref[...], h, (((1,), (1,)), ((), ())), preferred_element_type=F32)
            v_ref[:, rows] = _augment_values_t(v_t)
        else:
            z = seg(OFF_K)
            seqs = slice(r0 // k_ref.shape[1], (r0 + SUB_TILE) // k_ref.shape[1])
            k_ref[seqs] = z[:, :KV_WIDTH].reshape(k_ref[seqs].shape)
            v_ref[seqs] = z[:, KV_WIDTH:].reshape(v_ref[seqs].shape)
        for j in range(2 * D_MODEL // COL_CHUNK):
            z = seg(OFF_G + j * COL_CHUNK)
            gate_ref[rows, j * COL_CHUNK:(j + 1) * COL_CHUNK] = _sigmoid(z).astype(gate_ref.dtype)


def _in_projection(x, mod, layer, row_of_tile, g1, w_in, tm, *, w_v_t=None, rope_tabs=None, kv_cache=None):
    rows = x.shape[0]
    tile = lambda w: pl.BlockSpec((tm, w), lambda i: (i, 0))
    in_specs = [tile(D_MODEL), _mod_spec(layer, 0, row_of_tile), _mod_spec(layer, 1, row_of_tile),
                _resident((1, D_MODEL), layer), _resident((D_MODEL, IN_WIDTH), layer)]
    args = [x, mod, mod, g1, w_in]
    out_specs = [tile(FOURIER_WIDTH), tile(ATTN_WIDTH), None, None, tile(2 * D_MODEL)]
    out_shape = [jax.ShapeDtypeStruct((rows, FOURIER_WIDTH), BF16), jax.ShapeDtypeStruct((rows, ATTN_WIDTH), BF16),
                 None, None, jax.ShapeDtypeStruct((rows, 2 * D_MODEL), BF16)]
    latent = rope_tabs is not None
    aliases = {}
    if latent:
        tiles_per_seq = rope_tabs[0].shape[0] // tm
        in_specs += [_resident((KV_WIDTH, D_MODEL), layer)]
        in_specs += [pl.BlockSpec((tm, LANES), lambda i: (i % tiles_per_seq, 0))] * 3
        args += [w_v_t] + list(rope_tabs)
        out_specs[2] = tile(KV_WIDTH)
        out_shape[2] = jax.ShapeDtypeStruct((rows, KV_WIDTH), BF16)
        out_specs[3] = pl.BlockSpec((N_KV_HEADS * AUG_ROWS, tm), lambda i: (0, i))
        out_shape[3] = jax.ShapeDtypeStruct((N_KV_HEADS * AUG_ROWS, rows), BF16)
    else:
        s_ctx = kv_cache[0].shape[2]
        assert tm % s_ctx == 0
        in_specs += [pl.BlockSpec(memory_space=pl.ANY)] * 2
        args += list(kv_cache)
        aliases = {5: 2, 6: 3}
        for slot in (2, 3):
            out_specs[slot] = pl.BlockSpec((tm // s_ctx, None, s_ctx, KV_WIDTH), lambda i: (i, layer, 0, 0))
            out_shape[slot] = jax.ShapeDtypeStruct(kv_cache[0].shape, F32)
    return pl.pallas_call(
        functools.partial(_inproj_kernel, rope=latent),
        grid=(rows // tm,),
        in_specs=in_specs,
        out_specs=out_specs,
        out_shape=out_shape,
        input_output_aliases=aliases,
        compiler_params=_params(1),
        name="in_projection",
    )(*args)


def _softmax_head(s_ref, p_ref, sink):
    n_keys = s_ref.shape[0]
    blocks = [s_ref[r:r + MAX_ROWS, :] for r in range(0, n_keys, MAX_ROWS)]
    m = jnp.maximum(jnp.max(functools.reduce(jnp.maximum, blocks), axis=0, keepdims=True), sink)
    for r in range(0, n_keys, EXP_ROWS):
        rows = slice(r, r + EXP_ROWS)
        p_ref[rows, :] = jnp.exp2(s_ref[rows, :] - m).astype(BF16)
    return jnp.exp2(sink - m)


def _attend(q_of, key_parts, bias_t_of, n_biased, sink_of, s_ref, p_ref, o_ref):
    t = o_ref.shape[0]

    def score(kv):
        qs = jnp.concatenate([q_of(kv * Q_PER_KV + g) for g in range(Q_PER_KV)], axis=0)
        row = 0
        for k_of, _ in key_parts:
            k = k_of(kv)
            rows = slice(row, row + k.shape[0])
            part = lax.dot_general(k, qs, (((1,), (1,)), ((), ())), preferred_element_type=F32)
            for g in range(Q_PER_KV):
                head = part[:, g * t:(g + 1) * t]
                s_ref[kv, g, rows, :] = head + bias_t_of(rows) if row < n_biased else head
            row += k.shape[0]

    score(0)
    for kv in range(N_KV_HEADS):
        if kv + 1 < N_KV_HEADS:
            score(kv + 1)
        sink_terms = []
        for g in range(Q_PER_KV):
            sink = sink_of(kv * Q_PER_KV + g) * LOG2E
            sink_terms.append(_softmax_head(s_ref.at[kv, g], p_ref.at[kv, g], sink))
        o_aug, row = None, 0
        for _, v_t_of in key_parts:
            v_t = v_t_of(kv)
            rows = slice(row, row + v_t.shape[1])
            p = jnp.concatenate([p_ref[kv, g, rows, :] for g in range(Q_PER_KV)], axis=1)
            part = _dot(v_t, p)
            o_aug = part if o_aug is None else o_aug + part
            row += v_t.shape[1]
        denom = o_aug[HEAD_DIM:HEAD_DIM + 1, :] + jnp.concatenate(sink_terms, axis=1)
        o_t = o_aug[:HEAD_DIM, :] / denom
        for pair in range(Q_PER_KV // 2):
            two = jnp.concatenate([o_t[:, (2 * pair) * t:(2 * pair + 1) * t],
                                   o_t[:, (2 * pair + 1) * t:(2 * pair + 2) * t]], axis=0)
            col = (kv * (Q_PER_KV // 2) + pair) * LANES
            o_ref[:, col:col + LANES] = two.T.astype(o_ref.dtype)


def _head_cols(head):
    return slice(head * HEAD_DIM, (head + 1) * HEAD_DIM)


def _attn_ctx_kernel(sink_ref, q_ref, k_ref, v_ref, o_ref, s_ref, p_ref, *, layer):
    for b in range(q_ref.shape[0]):
        k = k_ref[b].astype(BF16)
        v_t = _augment_values_t(v_ref[b].T)
        _attend(lambda h: q_ref[b, :, _head_cols(h)],
                [(lambda kv: k[:, _head_cols(kv)], lambda kv: v_t[kv * AUG_ROWS:(kv + 1) * AUG_ROWS])],
                None, 0, lambda h: sink_ref[layer, h], s_ref.at[b % SCRATCH_SETS], p_ref.at[b % SCRATCH_SETS], o_ref.at[b])


def _context_attention(q, k, v, sink, layer):
    b, s, _ = q.shape
    nb = CTX_SEQS_PER_STEP
    assert s % MAX_ROWS == 0 and b % nb == 0
    blk = pl.BlockSpec((nb, s, ATTN_WIDTH), lambda i: (i, 0, 0))
    cache = pl.BlockSpec((nb, None, s, KV_WIDTH), lambda i: (i, layer, 0, 0))
    return pl.pallas_call(
        functools.partial(_attn_ctx_kernel, layer=layer),
        grid=(b // nb,),
        in_specs=[pl.BlockSpec(memory_space=pltpu.SMEM), blk, cache, cache],
        out_specs=blk,
        out_shape=jax.ShapeDtypeStruct((b, s, ATTN_WIDTH), BF16),
        scratch_shapes=[pltpu.VMEM((SCRATCH_SETS, N_KV_HEADS, Q_PER_KV, s, s), F32),
                        pltpu.VMEM((SCRATCH_SETS, N_KV_HEADS, Q_PER_KV, s, s), BF16)],
        compiler_params=_params(1),
        name="context_attention",
    )(sink, q, k, v)


def _window_bias_t():
    c = np.arange(KEY_SPAN)[None, :, None]
    r = np.arange(Q_BLOCK)[None, None, :]
    p = np.arange(KEY_SPAN // Q_BLOCK)[:, None, None]
    return jnp.asarray(np.where(np.abs(c - (r + p * Q_BLOCK)) <= WINDOW, 0.0, NEG_INF).astype(np.float32))


def _attn_lat_kernel(sink_ref, q_ref, k_ref, vt_ref, kc_ref, vc_ref, bias_ref, o_ref, kcb_ref, vcb_ref, s_ref, p_ref,
                     *, layer):
    t = Q_BLOCK
    s_len = k_ref.shape[0]

    @pl.when(pl.program_id(1) == 0)
    def _():
        kcb_ref[...] = kc_ref[...].astype(BF16)
        vcb_ref[...] = _augment_values_t(vc_ref[...].T)

    context = (lambda kv: kcb_ref[:, _head_cols(kv)], lambda kv: vcb_ref[kv * AUG_ROWS:(kv + 1) * AUG_ROWS, :])
    for j in range(q_ref.shape[0] // t):
        i = pl.program_id(1) * (q_ref.shape[0] // t) + j
        start = pl.multiple_of(jnp.clip((i - 1) * t, 0, s_len - KEY_SPAN), t)
        pattern = lax.div(i * t - start, t)
        window = (lambda kv, start=start: k_ref[pl.ds(start, KEY_SPAN), _head_cols(kv)],
                  lambda kv, start=start: vt_ref[kv * AUG_ROWS:(kv + 1) * AUG_ROWS, pl.ds(start, KEY_SPAN)])
        qrows = slice(j * t, (j + 1) * t)
        _attend(lambda h, qrows=qrows: q_ref[qrows, _head_cols(h)], [window, context],
                lambda rows, pattern=pattern: bias_ref[pattern, rows, :], KEY_SPAN,
                lambda h: sink_ref[layer, h], s_ref.at[j % SCRATCH_SETS], p_ref.at[j % SCRATCH_SETS], o_ref.at[qrows])


def _latent_attention(q, k, v_t, cache_k, cache_v, sink, layer):
    b, s, _ = q.shape
    past = cache_k.shape[2]
    nq = LATENT_QBLOCKS_PER_STEP
    assert s % (nq * Q_BLOCK) == 0 and s >= KEY_SPAN and WINDOW <= Q_BLOCK and Q_BLOCK == LANES
    assert past % MAX_ROWS == 0
    n_keys = KEY_SPAN + past
    qblk = pl.BlockSpec((None, nq * Q_BLOCK, ATTN_WIDTH), lambda bi, i: (bi, i, 0))
    cache = pl.BlockSpec((None, None, past, KV_WIDTH), lambda bi, i: (bi, layer, 0, 0))
    bias = _window_bias_t()
    return pl.pallas_call(
        functools.partial(_attn_lat_kernel, layer=layer),
        grid=(b, s // (nq * Q_BLOCK)),
        in_specs=[pl.BlockSpec(memory_space=pltpu.SMEM), qblk,
                  pl.BlockSpec((None, s, KV_WIDTH), lambda bi, i: (bi, 0, 0)),
                  pl.BlockSpec((N_KV_HEADS * AUG_ROWS, s), lambda bi, i: (0, bi)),
                  cache, cache, _resident(bias.shape)],
        out_specs=qblk,
        out_shape=jax.ShapeDtypeStruct((b, s, ATTN_WIDTH), BF16),
        scratch_shapes=[pltpu.VMEM((past, KV_WIDTH), BF16), pltpu.VMEM((N_KV_HEADS * AUG_ROWS, past), BF16),
                        pltpu.VMEM((SCRATCH_SETS, N_KV_HEADS, Q_PER_KV, n_keys, Q_BLOCK), F32),
                        pltpu.VMEM((SCRATCH_SETS, N_KV_HEADS, Q_PER_KV, n_keys, Q_BLOCK), BF16)],
        compiler_params=_params(2),
        name="latent_attention",
    )(sink, q, k, v_t, cache_k, cache_v, bias)


def _dft_tables(s):
    def cos_sin(n, rows):
        k = np.arange(rows, dtype=np.int64)[:, None]
        j = np.arange(n, dtype=np.int64)[None, :]
        ang = 2.0 * np.pi * ((k * j) % n) / n
        scale = n ** -0.5
        return (np.cos(ang) * scale).astype(np.float32), (np.sin(ang) * scale).astype(np.float32)
    as_bf16 = lambda t: jnp.asarray(t).astype(BF16)
    cc, sc = cos_sin(FOURIER_GROUP_DIM, FOURIER_GROUP_DIM)
    chan = as_bf16(np.concatenate([cc, sc], axis=1))
    if s < SYMMETRIC_MIN_SEQ:
        cs, ss = cos_sin(s, s)
        return chan, as_bf16(cs), as_bf16(ss)
    half = s // 2
    cs, ss = cos_sin(s, half + TABLE_PAD)
    rev = np.zeros((half, half), np.float32)
    rev[np.arange(1, half), half - np.arange(1, half)] = 1.0
    return chan, as_bf16(cs), as_bf16(ss[:half]), as_bf16(rev)


def _fourier_kernel(x_ref, chan_ref, cos_ref, sin_ref, *rest, row_chunk):
    *rev_ref, o_ref, ab_ref, eo_ref = rest
    nb, s, _ = x_ref.shape
    n_direct = sin_ref.shape[0]
    gd = FOURIER_GROUP_DIM
    for b in range(nb):
        for r in range(0, s, row_chunk):
            for g in range(N_FOURIER_GROUPS):
                ab = _dot(x_ref[b, r:r + row_chunk, g * gd:(g + 1) * gd], chan_ref[...])
                ab_ref[r:r + row_chunk, g * gd:(g + 1) * gd] = ab[:, :gd].astype(BF16)
                ab_ref[s + r:s + r + row_chunk, g * gd:(g + 1) * gd] = ab[:, gd:].astype(BF16)
        for r in range(0, n_direct, row_chunk):
            rows = slice(r, r + row_chunk)
            e = _dot(cos_ref[rows, :], ab_ref[:s, :])
            o = _dot(sin_ref[rows, :], ab_ref[s:, :])
            o_ref[b, rows, :] = (e - o).astype(o_ref.dtype)
            if rev_ref:
                eo_ref[rows, :] = (e + o).astype(BF16)
        if rev_ref:
            e_mid = _dot(cos_ref[n_direct:n_direct + TABLE_PAD, :], ab_ref[:s, :])
            first = lax.broadcasted_iota(jnp.int32, e_mid.shape, 0) == 0
            for r in range(0, n_direct, row_chunk):
                up = _dot(rev_ref[0][r:r + row_chunk, :], eo_ref[...])
                if r == 0:
                    up = jnp.concatenate([up[:TABLE_PAD] + jnp.where(first, e_mid, 0.0), up[TABLE_PAD:]], axis=0)
                o_ref[b, n_direct + r:n_direct + r + row_chunk, :] = up.astype(o_ref.dtype)


def _fourier_mix(u, tables, seqs_per_step):
    b, s, w = u.shape
    n_direct = tables[2].shape[0]
    row_chunk = min(n_direct, 512)
    blk = pl.BlockSpec((seqs_per_step, s, w), lambda i: (i, 0, 0))
    return pl.pallas_call(
        functools.partial(_fourier_kernel, row_chunk=row_chunk),
        grid=(b // seqs_per_step,),
        in_specs=[blk] + [_resident(t.shape) for t in tables],
        out_specs=blk,
        out_shape=jax.ShapeDtypeStruct((b, s, w), BF16),
        scratch_shapes=[pltpu.VMEM((2 * s, w), BF16), pltpu.VMEM((n_direct, w), BF16)],
        compiler_params=_params(1),
        name="fourier_mix",
    )(u, *tables)


def _merge_mlp_kernel(*refs, final):
    (x_ref, f_ref, a_ref, gate_ref, ga1_ref, sh2_ref, sc2_ref, ga2_ref, g2_ref,
     wfo_ref, wao_ref, wout_ref, w1_ref, w2_ref) = refs[:14]
    o_ref = refs[-1]
    gate_f = gate_ref[:, :D_MODEL].astype(F32)
    gate_a = gate_ref[:, D_MODEL:].astype(F32)
    m = gate_f * _dot(f_ref[...], wfo_ref[...]) + gate_a * _dot(a_ref[...], wao_ref[...])
    x1 = x_ref[...] + ga1_ref[...] * _dot(m.astype(BF16), wout_ref[...])
    h = (_rmsnorm(x1, g2_ref[...]) * (1.0 + sc2_ref[...]) + sh2_ref[...]).astype(BF16)
    ff = None
    for j in range(D_FF // COL_CHUNK):
        cs = slice(j * COL_CHUNK, (j + 1) * COL_CHUNK)
        t = jnp.maximum(_dot(h, w1_ref[:, cs]), 0.0)
        part = _dot((t * t).astype(BF16), w2_ref[cs, :])
        ff = part if ff is None else ff + part
    x2 = x1 + ga2_ref[...] * ff
    if final:
        x2 = _rmsnorm(x2, refs[14][...])
    o_ref[...] = x2


def _merge_mlp(x, f, attn, gates, mod, layer, row_of_tile, g2, w_fo, w_ao, w_out, w_ff1, w_ff2, final_g, tm):
    rows = x.shape[0]
    tile = lambda w: pl.BlockSpec((tm, w), lambda i: (i, 0))
    in_specs = [tile(D_MODEL), tile(FOURIER_WIDTH), tile(ATTN_WIDTH), tile(2 * D_MODEL)]
    in_specs += [_mod_spec(layer, slot, row_of_tile) for slot in (2, 3, 4, 5)]
    in_specs += [_resident((1, D_MODEL), layer)]
    in_specs += [_resident(w.shape[1:], layer) for w in (w_fo, w_ao, w_out, w_ff1, w_ff2)]
    args = [x, f, attn, gates, mod, mod, mod, mod, g2, w_fo, w_ao, w_out, w_ff1, w_ff2]
    if final_g is not None:
        in_specs.append(_resident((1, D_MODEL)))
        args.append(final_g)
    return pl.pallas_call(
        functools.partial(_merge_mlp_kernel, final=final_g is not None),
        grid=(rows // tm,),
        in_specs=in_specs,
        out_specs=tile(D_MODEL),
        out_shape=jax.ShapeDtypeStruct((rows, D_MODEL), F32),
        compiler_params=_params(1),
        name="merge_mlp",
    )(*args)


def _rope_tables(n_tokens):
    t = jnp.arange(n_tokens)
    row = (t // GRID_W).astype(F32)
    col = (t % GRID_W).astype(F32)
    inv = ROPE_THETA ** (-jnp.arange(ROPE_PAIRS_PER_AXIS, dtype=F32) / ROPE_PAIRS_PER_AXIS)
    ang = jnp.concatenate([row[:, None] * inv[None, :], col[:, None] * inv[None, :]], axis=-1)
    cos = jnp.repeat(jnp.cos(ang), 2, axis=1)
    sin = jnp.repeat(jnp.sin(ang), 2, axis=1)
    even = (jnp.arange(HEAD_DIM) % 2 == 0)[None, :]
    reps = LANES // HEAD_DIM
    return (jnp.tile(cos, (1, reps)),
            jnp.tile(jnp.where(even, -sin, 0.0), (1, reps)),
            jnp.tile(jnp.where(even, 0.0, sin), (1, reps)))


def kernel(x_prompt, x_sample, c, cache_k, cache_v, c_ctx, w_ada, b_ada, norm1_g, norm2_g,
           w_in, sink, w_fo, w_ao, w_out, w_ff1, w_ff2, final_g):
    depth = w_in.shape[0]
    b_ctx, s_ctx, _ = x_prompt.shape
    b_lat, s_lat, _ = x_sample.shape
    past = cache_k.shape[2]
    assert 1 + b_lat <= MOD_ROWS
    tm, tm_in = ROW_TILE, INPROJ_ROW_TILE
    assert (b_ctx * s_ctx) % tm == 0 and s_lat % tm == 0 and (b_ctx * s_ctx) % tm_in == 0 and s_lat % tm_in == 0
    assert tm_in % SUB_TILE == 0 and SUB_TILE % s_ctx == 0

    cvecs = jnp.zeros((MOD_ROWS, D_MODEL), F32).at[0].set(c_ctx).at[1:1 + b_lat].set(c)
    mod = _modulation(cvecs, w_ada, b_ada).reshape(depth, MOD_ROWS, 6, 1, D_MODEL)
    ctx_row = lambda tile: lambda i: 0
    lat_row = lambda tile: lambda i: 1 + i // (s_lat // tile)

    w_in_b, w_fo_b, w_ao_b, w_out_b, w_ff1_b, w_ff2_b = (
        w.astype(BF16) for w in (w_in, w_fo, w_ao, w_out, w_ff1, w_ff2))
    w_v_t = jnp.swapaxes(w_in_b[:, :, OFF_V:OFF_G], 1, 2)
    g1 = norm1_g.reshape(depth, 1, D_MODEL)
    g2 = norm2_g.reshape(depth, 1, D_MODEL)
    gf = final_g.reshape(1, D_MODEL)
    rope_tabs = _rope_tables(s_lat)
    dft_ctx = _dft_tables(s_ctx)
    dft_lat = _dft_tables(s_lat)
    cache_k = cache_k.reshape(b_lat, depth, past, KV_WIDTH)
    cache_v = cache_v.reshape(b_lat, depth, past, KV_WIDTH)

    def layer_tail(x, uf, attn, gates, l, row_of_tile, fourier_tabs, batch, seqs_per_step):
        seq = x.shape[0] // batch
        f = _fourier_mix(uf.reshape(batch, seq, FOURIER_WIDTH), fourier_tabs, seqs_per_step)
        return _merge_mlp(x, f.reshape(-1, FOURIER_WIDTH), attn.reshape(-1, ATTN_WIDTH), gates, mod, l, row_of_tile,
                          g2, w_fo_b, w_ao_b, w_out_b, w_ff1_b, w_ff2_b, gf if l == depth - 1 else None, tm)

    xc = x_prompt.reshape(b_ctx * s_ctx, D_MODEL)
    kv_cache = (jnp.zeros((b_ctx, depth, s_ctx, KV_WIDTH), F32),) * 2
    for l in range(depth):
        uf, q, *kv_cache, gates = _in_projection(xc, mod, l, ctx_row(tm_in), g1, w_in_b, tm_in, kv_cache=kv_cache)
        attn = _context_attention(q.reshape(b_ctx, s_ctx, ATTN_WIDTH), *kv_cache, sink, l)
        xc = layer_tail(xc, uf, attn, gates, l, ctx_row(tm), dft_ctx, b_ctx, 4 if b_ctx % 4 == 0 else 1)
    y_prompt = xc.reshape(b_ctx, s_ctx, D_MODEL)
    new_k, new_v = (buf.reshape(b_ctx, depth, s_ctx, N_KV_HEADS, HEAD_DIM) for buf in kv_cache)

    xs = x_sample.reshape(b_lat * s_lat, D_MODEL)
    for l in range(depth):
        uf, q, k, v_t, gates = _in_projection(xs, mod, l, lat_row(tm_in), g1, w_in_b, tm_in, w_v_t=w_v_t,
                                              rope_tabs=rope_tabs)
        attn = _latent_attention(q.reshape(b_lat, s_lat, ATTN_WIDTH), k.reshape(b_lat, s_lat, KV_WIDTH),
                                 v_t, cache_k, cache_v, sink, l)
        xs = layer_tail(xs, uf, attn, gates, l, lat_row(tm), dft_lat, b_lat, 1)
    y_sample = xs.reshape(b_lat, s_lat, D_MODEL)

    return y_prompt, y_sample, new_k, new_v
```
